```python
import math, functools
import jax, jax.numpy as jnp
from jax import lax
import numpy as np

D_MODEL = 1024
BATCH = 1
SEQ = 16384
DEPTH = 1
DEC_BATCH = 128
DEC_SEQ = 4
PAST_LEN = 16384
PAGE_SIZE = 128

D_CONV = D_MODEL // 2
CONV_W = 3
N_HEADS = 8
QK_NOPE = 64
QK_ROPE = 32
V_DIM = 64
D_ATTN = N_HEADS * V_DIM
Q_LORA = 384
KV_LORA = 256
D_MIX = D_CONV + D_ATTN
D_IN = 3 * D_CONV + Q_LORA + KV_LORA + QK_ROPE
ROPE_THETA = 10000.0
SOFTMAX_SCALE = (QK_NOPE + QK_ROPE) ** -0.5
Q_BLOCK = 128
NEG = -1e30
N_EXPERTS = 32
TOP_K = 4
D_FF = D_MODEL
SWIGLU_LIMIT = 7.0
SWIGLU_ALPHA = 1.702
RMS_EPS = 1e-6

kernel_name = "hymba_conv_mla_moe_step"


def rmsnorm(x, g):
    xf = x.astype(jnp.float32)
    out = xf * lax.rsqrt(jnp.mean(xf * xf, axis=-1, keepdims=True) + RMS_EPS)
    return (out * g.astype(jnp.float32)).astype(x.dtype)


def apply_rope(x, pos):
    half = QK_ROPE // 2
    inv_freq = ROPE_THETA ** (-jnp.arange(half, dtype=jnp.float32) / half)
    ang = pos.astype(jnp.float32)[:, None] * inv_freq[None, :]
    ang = ang.reshape(ang.shape[:1] + (1,) * (x.ndim - 3) + (half,))
    cos, sin = jnp.cos(ang), jnp.sin(ang)
    x1 = x[..., :half].astype(jnp.float32)
    x2 = x[..., half:].astype(jnp.float32)
    return jnp.concatenate([x1 * cos - x2 * sin, x1 * sin + x2 * cos], axis=-1).astype(x.dtype)


def mixer_front(h, pos, norm_g, w_in, q_norm_g, kv_norm_g, w_uq):
    n = rmsnorm(h, norm_g)
    z = n @ w_in
    cuts = [D_CONV, 2 * D_CONV, 3 * D_CONV, 3 * D_CONV + Q_LORA, 3 * D_CONV + Q_LORA + KV_LORA]
    b_g, c_g, x_c, z_q, z_kv, z_kr = jnp.split(z, cuts, axis=-1)
    bsz, s = h.shape[0], h.shape[1]
    q = (rmsnorm(z_q, q_norm_g) @ w_uq).reshape(bsz, s, N_HEADS, QK_NOPE + QK_ROPE)
    q_nope = q[..., :QK_NOPE]
    q_rope = apply_rope(q[..., QK_NOPE:], pos)
    c_kv = rmsnorm(z_kv, kv_norm_g)
    k_rope = apply_rope(z_kr, pos)
    return b_g, c_g * x_c, q_nope, q_rope, c_kv, k_rope


def short_conv(u, buf, conv_w):
    full = jnp.concatenate([buf, u], axis=1)
    s = u.shape[1]
    y = full[:, 0:s] * conv_w[0]
    for k in range(1, CONV_W):
        y = y + full[:, k:k + s] * conv_w[k]
    return y, full[:, -(CONV_W - 1):]


def split_ukv(w_ukv_l):
    w = w_ukv_l.reshape(KV_LORA, N_HEADS, QK_NOPE + V_DIM)
    return w[..., :QK_NOPE], w[..., QK_NOPE:]


def attend_prompt(q_nope, q_rope, c_kv, k_rope, w_ukv_l):
    bsz, s = q_nope.shape[0], q_nope.shape[1]
    w_uk, w_uv = split_ukv(w_ukv_l)
    k_nope = jnp.einsum('bsc,chd->bshd', c_kv, w_uk)
    v = jnp.einsum('bsc,chd->bshd', c_kv, w_uv)
    nb = s // Q_BLOCK
    qn_b = jnp.moveaxis(q_nope.reshape(bsz, nb, Q_BLOCK, N_HEADS, QK_NOPE), 1, 0)
    qr_b = jnp.moveaxis(q_rope.reshape(bsz, nb, Q_BLOCK, N_HEADS, QK_ROPE), 1, 0)
    key_pos = jnp.arange(s, dtype=jnp.int32)

    def block(args):
        qn, qr, i = args
        sc = (jnp.einsum('bqhd,bkhd->bhqk', qn, k_nope)
              + jnp.einsum('bqhr,bkr->bhqk', qr, k_rope)).astype(jnp.float32) * SOFTMAX_SCALE
        q_pos = i * Q_BLOCK + jnp.arange(Q_BLOCK, dtype=jnp.int32)
        mask = key_pos[None, :] <= q_pos[:, None]
        p = jax.nn.softmax(jnp.where(mask, sc, NEG), axis=-1).astype(v.dtype)
        return jnp.einsum('bhqk,bkhd->bqhd', p, v)

    o = lax.map(block, (qn_b, qr_b, jnp.arange(nb, dtype=jnp.int32)))
    return jnp.moveaxis(o, 0, 1).reshape(bsz, s, D_ATTN)


def attend_sample(q_nope, q_rope, c_kv, k_rope, past_c, past_kr, w_ukv_l):
    bsz, t = q_nope.shape[0], q_nope.shape[1]
    p_len = past_c.shape[1]
    w_uk, w_uv = split_ukv(w_ukv_l)
    q_lat = jnp.einsum('bthd,chd->bthc', q_nope, w_uk)
    s_past = jnp.einsum('bthc,bkc->bhtk', q_lat, past_c) + jnp.einsum('bthr,bkr->bhtk', q_rope, past_kr)
    s_new = jnp.einsum('bthc,bkc->bhtk', q_lat, c_kv) + jnp.einsum('bthr,bkr->bhtk', q_rope, k_rope)
    sc = jnp.concatenate([s_past, s_new], axis=-1).astype(jnp.float32) * SOFTMAX_SCALE
    mask = jnp.concatenate([jnp.ones((t, p_len), bool), jnp.tril(jnp.ones((t, t), bool))], axis=-1)
    p = jax.nn.softmax(jnp.where(mask, sc, NEG), axis=-1).astype(c_kv.dtype)
    o_lat = (jnp.einsum('bhtk,bkc->bthc', p[..., :p_len], past_c)
             + jnp.einsum('bhtk,bkc->bthc', p[..., p_len:], c_kv))
    return jnp.einsum('bthc,chd->bthd', o_lat, w_uv).reshape(bsz, t, D_ATTN)


def mixer_back(h, y_conv, y_attn, g_conv, g_attn, w_out):
    y = jnp.concatenate([rmsnorm(y_conv, g_conv), rmsnorm(y_attn, g_attn)], axis=-1)
    return h + y @ w_out


def moe(x, w_router, b_router, w_gate_up, b_gate_up, w_down, b_down):
    logits = (x @ w_router + b_router).astype(jnp.float32)
    top_v, top_i = lax.top_k(logits, TOP_K)
    gates = jax.nn.softmax(top_v, axis=-1).astype(x.dtype)
    dense_gate = jnp.einsum('bsk,bske->bse', gates, jax.nn.one_hot(top_i, N_EXPERTS, dtype=x.dtype))
    out = jnp.zeros_like(x)
    for e in range(N_EXPERTS):
        gu = x @ w_gate_up[e] + b_gate_up[e]
        gate = jnp.minimum(gu[..., :D_FF], SWIGLU_LIMIT)
        up = jnp.clip(gu[..., D_FF:], -SWIGLU_LIMIT, SWIGLU_LIMIT)
        hid = (up + 1.0) * gate * jax.nn.sigmoid(SWIGLU_ALPHA * gate)
        out = out + dense_gate[..., e:e + 1] * (hid @ w_down[e] + b_down[e])
    return out


def setup_inputs(seed: int = 0) -> dict:
    key = jax.random.key(seed)
    ks = jax.random.split(key, 24)
    f32 = jnp.float32
    n_pages = PAST_LEN // PAGE_SIZE
    n_used = DEC_BATCH * n_pages
    n_phys = n_used + n_used // 4

    def nrm(k, shape, scale):
        return jax.random.normal(k, shape, f32) * scale

    def gain(k, shape):
        return 1.0 + 0.02 * jax.random.normal(k, shape, f32)

    page_table = jax.random.permutation(ks[5], n_phys)[:n_used].reshape(DEC_BATCH, n_pages).astype(jnp.int32)
    return {
        "x_prompt": nrm(ks[0], (BATCH, SEQ, D_MODEL), 1.0),
        "x_sample": nrm(ks[1], (DEC_BATCH, DEC_SEQ, D_MODEL), 1.0),
        "cache_kv_latent": nrm(ks[2], (DEPTH, n_phys, PAGE_SIZE, KV_LORA), 1.0),
        "cache_k_rope": nrm(ks[3], (DEPTH, n_phys, PAGE_SIZE, QK_ROPE), 1.0),
        "state_conv": nrm(ks[4], (DEPTH, DEC_BATCH, CONV_W - 1, D_CONV), 0.5),
        "page_table": page_table,
        "norm_mix_g": gain(ks[6], (DEPTH, D_MODEL)),
        "w_in": nrm(ks[7], (DEPTH, D_MODEL, D_IN), D_MODEL ** -0.5),
        "conv_w": nrm(ks[8], (DEPTH, CONV_W, D_CONV), CONV_W ** -0.5),
        "q_norm_g": gain(ks[9], (DEPTH, Q_LORA)),
        "kv_norm_g": gain(ks[10], (DEPTH, KV_LORA)),
        "w_uq": nrm(ks[11], (DEPTH, Q_LORA, N_HEADS * (QK_NOPE + QK_ROPE)), Q_LORA ** -0.5),
        "w_ukv": nrm(ks[12], (DEPTH, KV_LORA, N_HEADS * (QK_NOPE + V_DIM)), KV_LORA ** -0.5),
        "out_norm_conv_g": gain(ks[13], (DEPTH, D_CONV)),
        "out_norm_attn_g": gain(ks[14], (DEPTH, D_ATTN)),
        "w_out": nrm(ks[15], (DEPTH, D_MIX, D_MODEL), D_MIX ** -0.5),
        "norm_ffn_g": gain(ks[16], (DEPTH, D_MODEL)),
        "w_router": nrm(ks[17], (DEPTH, D_MODEL, N_EXPERTS), D_MODEL ** -0.5),
        "b_router": nrm(ks[18], (DEPTH, N_EXPERTS), 0.01),
        "w_gate_up": nrm(ks[19], (DEPTH, N_EXPERTS, D_MODEL, 2 * D_FF), D_MODEL ** -0.5),
        "b_gate_up": nrm(ks[20], (DEPTH, N_EXPERTS, 2 * D_FF), 0.01),
        "w_down": nrm(ks[21], (DEPTH, N_EXPERTS, D_FF, D_MODEL), D_FF ** -0.5),
        "b_down": nrm(ks[22], (DEPTH, N_EXPERTS, D_MODEL), 0.01),
        "final_norm_g": gain(ks[23], (D_MODEL,)),
    }


def reference(x_prompt, x_sample, cache_kv_latent, cache_k_rope, state_conv, page_table,
              norm_mix_g, w_in, conv_w, q_norm_g, kv_norm_g, w_uq, w_ukv,
              out_norm_conv_g, out_norm_attn_g, w_out, norm_ffn_g,
              w_router, b_router, w_gate_up, b_gate_up, w_down, b_down, final_norm_g):
    n_pages = PAST_LEN // PAGE_SIZE
    pos_p = jnp.arange(SEQ, dtype=jnp.int32)
    pos_s = PAST_LEN + jnp.arange(DEC_SEQ, dtype=jnp.int32)
    hp, hs = x_prompt, x_sample
    lat_p, kr_p, cst_p, lat_s, kr_s, cst_s = [], [], [], [], [], []
    for l in range(DEPTH):
        b_g, u, qn, qr, ckv, kr = mixer_front(hp, pos_p, norm_mix_g[l], w_in[l], q_norm_g[l], kv_norm_g[l], w_uq[l])
        y_c, buf = short_conv(u, jnp.zeros((BATCH, CONV_W - 1, D_CONV), u.dtype), conv_w[l])
        y_a = attend_prompt(qn, qr, ckv, kr, w_ukv[l])
        hp = mixer_back(hp, b_g * y_c, y_a, out_norm_conv_g[l], out_norm_attn_g[l], w_out[l])
        hp = hp + moe(rmsnorm(hp, norm_ffn_g[l]), w_router[l], b_router[l], w_gate_up[l], b_gate_up[l], w_down[l], b_down[l])
        lat_p.append(ckv)
        kr_p.append(kr)
        cst_p.append(buf)
        b_g, u, qn, qr, ckv, kr = mixer_front(hs, pos_s, norm_mix_g[l], w_in[l], q_norm_g[l], kv_norm_g[l], w_uq[l])
        y_c, buf = short_conv(u, state_conv[l], conv_w[l])
        past_c = cache_kv_latent[l][page_table].reshape(DEC_BATCH, n_pages * PAGE_SIZE, KV_LORA)
        past_kr = cache_k_rope[l][page_table].reshape(DEC_BATCH, n_pages * PAGE_SIZE, QK_ROPE)
        y_a = attend_sample(qn, qr, ckv, kr, past_c, past_kr, w_ukv[l])
        hs = mixer_back(hs, b_g * y_c, y_a, out_norm_conv_g[l], out_norm_attn_g[l], w_out[l])
        hs = hs + moe(rmsnorm(hs, norm_ffn_g[l]), w_router[l], b_router[l], w_gate_up[l], b_gate_up[l], w_down[l], b_down[l])
        lat_s.append(ckv)
        kr_s.append(kr)
        cst_s.append(buf)
    y_prompt = rmsnorm(hp, final_norm_g)
    y_sample = rmsnorm(hs, final_norm_g)
    return (y_prompt, y_sample, jnp.stack(lat_p), jnp.stack(kr_p), jnp.stack(cst_p),
            jnp.stack(lat_s), jnp.stack(kr_s), jnp.stack(cst_s))
```

```python
import functools

import jax
import jax.numpy as jnp
from jax import lax
from jax.experimental import pallas as pl
from jax.experimental.pallas import tpu as pltpu

N_HEADS = 8
QK_NOPE = 64
QK_ROPE = 32
V_DIM = 64
HALF_ROPE = QK_ROPE // 2
ROPE_THETA = 10000.0
SOFTMAX_SCALE = (QK_NOPE + QK_ROPE) ** -0.5
TOP_K = 4
SWIGLU_LIMIT = 7.0
SWIGLU_ALPHA = 1.702
RMS_EPS = 1e-6
NEG = -1e30

LANES = 128
MIB = 1024 * 1024
F32 = jnp.float32
BF16 = jnp.bfloat16

NT_DIMS = (((1,), (1,)), ((), ()))


def _rms(x, g):
    return x * lax.rsqrt(jnp.mean(x * x, axis=-1, keepdims=True) + RMS_EPS) * g


def _dot(a, b):
    return jnp.dot(a, b, preferred_element_type=F32)


def _dot_nt(a, b):
    return lax.dot_general(a, b, NT_DIMS, preferred_element_type=F32)


def _front_kernel(*refs, d_conv, q_lora, kv_lora, seq_rows, prompt):
    if prompt:
        (x_ref, tqc_ref, tqs_ref, tk_ref, gmix_ref, win_ref, cw_ref, gq_ref, gkv_ref, wq_ref, gconv_ref,
         wk_ref, place_ref, wv_ref,
         u_ref, ycn_ref, q_ref, ckv_ref, kr_ref, k_ref, v_ref, carry_ref) = refs
    else:
        (x_ref, tqc_ref, tqs_ref, tk_ref, ov1_ref, ov2_ref, gmix_ref, win_ref, cw_ref, gq_ref, gkv_ref, wq_ref,
         gconv_ref, wukt_ref,
         u_ref, ycn_ref, q_ref, ckv_ref, kr_ref, qlat_ref) = refs
    tm = x_ref.shape[0]
    n = _rms(x_ref[...], gmix_ref[...]).astype(BF16)
    z = _dot(n, win_ref[...])
    c0 = 3 * d_conv
    b_g = z[:, 0:d_conv]
    u = z[:, d_conv:2 * d_conv] * z[:, 2 * d_conv:c0]
    z_q = z[:, c0:c0 + q_lora]
    z_kv = z[:, c0 + q_lora:c0 + q_lora + kv_lora]
    z_r = z[:, c0 + q_lora + kv_lora:]
    u_ref[...] = u

    row = lax.broadcasted_iota(jnp.int32, u.shape, 0)
    r1 = pltpu.roll(u, 1, 0)
    r2 = pltpu.roll(u, 2, 0)
    if prompt:
        @pl.when(pl.program_id(0) == 0)
        def _():
            carry_ref[...] = jnp.zeros_like(carry_ref)
        last1 = carry_ref[7:8, :]
        last2 = carry_ref[6:7, :]
        p1 = jnp.where(row == 0, last1, r1)
        p2 = jnp.where(row == 0, last2, jnp.where(row == 1, last1, r2))
        carry_ref[...] = u[tm - 8:tm, :]
    else:
        t = row % seq_rows
        p1 = jnp.where(t == 0, ov1_ref[...], r1)
        p2 = jnp.where(t < 2, ov2_ref[...], r2)
    cw = cw_ref[...]
    y_c = p2 * cw[0:1, :] + p1 * cw[1:2, :] + u * cw[2:3, :]
    ycn_ref[...] = _rms(b_g * y_c, gconv_ref[...]).astype(BF16)

    nq = _rms(z_q, gq_ref[...]).astype(BF16)
    q2 = _dot(nq, wq_ref[...])
    hw = N_HEADS * LANES
    tqc = tqc_ref[...]
    tqs = tqs_ref[...]
    q_heads = []
    for h in range(N_HEADS):
        qa = q2[:, h * LANES:(h + 1) * LANES]
        qb = q2[:, hw + h * LANES:hw + (h + 1) * LANES]
        q_heads.append((qa * tqc + qb * tqs).astype(BF16))
    q = jnp.concatenate(q_heads, axis=1)
    q_ref[...] = q

    ckv = _rms(z_kv, gkv_ref[...])
    ckv_ref[...] = ckv
    zr = z_r * tk_ref[...]
    kr = zr[:, 0:QK_ROPE] + zr[:, QK_ROPE:2 * QK_ROPE]
    kr_ref[...] = kr

    if prompt:
        ckv_b = ckv.astype(BF16)
        k_ref[...] = (_dot(ckv_b, wk_ref[...]) + _dot(kr.astype(BF16), place_ref[...])).astype(BF16)
        v_ref[...] = _dot(ckv_b, wv_ref[...]).astype(BF16)
    else:
        for h in range(N_HEADS):
            qlat_ref[:, h * kv_lora:(h + 1) * kv_lora] = _dot(q_heads[h], wukt_ref[h]).astype(BF16)


def _front(x, pos, w, *, prompt, ov=None, seq_rows=0, tm=512):
    n, d = x.shape
    d_conv, q_lora, kv_lora = w["d_conv"], w["q_lora"], w["kv_lora"]
    tm = min(tm, n)
    hw = N_HEADS * LANES
    inv_freq = ROPE_THETA ** (-jnp.arange(HALF_ROPE, dtype=F32) / HALF_ROPE)
    ang = pos.astype(F32)[:, None] * inv_freq[None, :]
    cos, sin = jnp.cos(ang), jnp.sin(ang)
    zeros = lambda c: jnp.zeros((n, c), F32)
    tqc = jnp.concatenate([jnp.full((n, QK_NOPE), SOFTMAX_SCALE, F32), cos * SOFTMAX_SCALE, cos * SOFTMAX_SCALE,
                           zeros(LANES - QK_NOPE - QK_ROPE)], axis=1)
    tqs = jnp.concatenate([zeros(QK_NOPE), sin * SOFTMAX_SCALE, sin * SOFTMAX_SCALE,
                           zeros(LANES - QK_NOPE - QK_ROPE)], axis=1)
    tk = jnp.concatenate([cos, cos, sin, sin, zeros(LANES - 2 * QK_ROPE)], axis=1)

    row = lambda c: pl.BlockSpec((tm, c), lambda i: (i, 0))
    full = lambda a: pl.BlockSpec(a.shape, lambda i: (0,) * a.ndim)
    ins = [x, tqc, tqs, tk]
    in_specs = [row(d), row(LANES), row(LANES), row(LANES)]
    if not prompt:
        ins += [ov[0], ov[1]]
        in_specs += [row(d_conv), row(d_conv)]
    shared = [w["g_mix"], w["w_in"], w["conv_w"], w["g_q"], w["g_kv"], w["wq2"], w["g_conv"]]
    shared += [w["wk_pad"], w["place"], w["w_uv"]] if prompt else [w["wuk_t"]]
    ins += shared
    in_specs += [full(a) for a in shared]

    outs = [jax.ShapeDtypeStruct((n, d_conv), F32), jax.ShapeDtypeStruct((n, d_conv), BF16),
            jax.ShapeDtypeStruct((n, hw), BF16), jax.ShapeDtypeStruct((n, kv_lora), F32),
            jax.ShapeDtypeStruct((n, QK_ROPE), F32)]
    out_specs = [row(d_conv), row(d_conv), row(hw), row(kv_lora), row(QK_ROPE)]
    scratch = []
    if prompt:
        outs += [jax.ShapeDtypeStruct((n, hw), BF16), jax.ShapeDtypeStruct((n, N_HEADS * V_DIM), BF16)]
        out_specs += [row(hw), row(N_HEADS * V_DIM)]
        scratch = [pltpu.VMEM((8, d_conv), F32)]
    else:
        outs += [jax.ShapeDtypeStruct((n, N_HEADS * kv_lora), BF16)]
        out_specs += [row(N_HEADS * kv_lora)]
    kern = functools.partial(_front_kernel, d_conv=d_conv, q_lora=q_lora, kv_lora=kv_lora, seq_rows=seq_rows,
                             prompt=prompt)
    return pl.pallas_call(
        kern, out_shape=outs, grid=(n // tm,), in_specs=in_specs, out_specs=out_specs, scratch_shapes=scratch,
        name="front_prompt" if prompt else "front_sample",
        compiler_params=pltpu.CompilerParams(dimension_semantics=("arbitrary",), vmem_limit_bytes=56 * MIB),
    )(*ins)


def _flash_kernel(q_ref, k_ref, v_ref, o_ref, m_sc, l_sc, acc_sc, *, tq):
    i = pl.program_id(1)
    m_sc[...] = jnp.full_like(m_sc, NEG)
    l_sc[...] = jnp.zeros_like(l_sc)
    acc_sc[...] = jnp.zeros_like(acc_sc)
    q = q_ref[...]

    def step(j, masked):
        start = pl.multiple_of(j * tq, tq)
        ks = k_ref[pl.ds(start, tq), :]
        vs = v_ref[pl.ds(start, tq), :]
        for h in range(2):
            s = _dot_nt(q[:, h * LANES:(h + 1) * LANES], ks[:, h * LANES:(h + 1) * LANES])
            if masked:
                r = lax.broadcasted_iota(jnp.int32, s.shape, 0)
                c = lax.broadcasted_iota(jnp.int32, s.shape, 1)
                s = jnp.where(c <= r, s, NEG)
            m_prev = m_sc[h]
            m_new = jnp.maximum(m_prev, jnp.max(s, axis=-1, keepdims=True))
            alpha = jnp.exp(m_prev - m_new)
            p = jnp.exp(s - m_new)
            l_sc[h] = alpha * l_sc[h] + jnp.sum(p, axis=-1, keepdims=True)
            acc_sc[h] = alpha * acc_sc[h] + _dot(p.astype(BF16), vs)
            m_sc[h] = m_new

    def body(j, carry):
        step(j, False)
        return carry

    lax.fori_loop(0, i, body, 0)
    step(i, True)
    lane = lax.broadcasted_iota(jnp.int32, o_ref.shape, 1)
    o_ref[...] = jnp.where(lane < V_DIM, acc_sc[0] / l_sc[0], acc_sc[1] / l_sc[1])


def _flash(q, k, v, *, tq=512):
    s = q.shape[0]
    tq = min(tq, s)
    pairs = N_HEADS // 2
    return pl.pallas_call(
        functools.partial(_flash_kernel, tq=tq),
        out_shape=jax.ShapeDtypeStruct((s, N_HEADS * V_DIM), F32),
        grid=(pairs, s // tq),
        in_specs=[pl.BlockSpec((tq, 2 * LANES), lambda p, i: (i, p)),
                  pl.BlockSpec((s, 2 * LANES), lambda p, i: (0, p)),
                  pl.BlockSpec((s, 2 * V_DIM), lambda p, i: (0, p))],
        out_specs=pl.BlockSpec((tq, 2 * V_DIM), lambda p, i: (i, p)),
        scratch_shapes=[pltpu.VMEM((2, tq, 1), F32), pltpu.VMEM((2, tq, 1), F32),
                        pltpu.VMEM((2, tq, 2 * V_DIM), F32)],
        name="flash_prompt",
        compiler_params=pltpu.CompilerParams(dimension_semantics=("arbitrary", "arbitrary"),
                                             vmem_limit_bytes=56 * MIB),
    )(q, k, v)


def _paged_kernel(pt_ref, qlat_ref, qr_ref, cnew_ref, krnew_ref, *rest, pps, page, t_new):
    del pt_ref
    c_refs = rest[:pps]
    kr_refs = rest[pps:2 * pps]
    o_ref, m_sc, l_sc, acc_sc, cbuf, krbuf = rest[2 * pps:]
    c = pl.program_id(1)

    @pl.when(c == 0)
    def _():
        m_sc[...] = jnp.full_like(m_sc, NEG)
        l_sc[...] = jnp.zeros_like(l_sc)
        acc_sc[...] = jnp.zeros_like(acc_sc)

    for p in range(pps):
        cbuf[p * page:(p + 1) * page, :] = c_refs[p][...].astype(BF16)
        krbuf[p * page:(p + 1) * page, :] = kr_refs[p][...].astype(BF16)
    qlat = qlat_ref[...]
    qr = qr_ref[...]
    cb = cbuf[...]
    s = _dot_nt(qlat, cb) + _dot_nt(qr, krbuf[...])
    m_prev = m_sc[...]
    m_new = jnp.maximum(m_prev, jnp.max(s, axis=-1, keepdims=True))
    alpha = jnp.exp(m_prev - m_new)
    p_ = jnp.exp(s - m_new)
    l_sc[...] = alpha * l_sc[...] + jnp.sum(p_, axis=-1, keepdims=True)
    acc_sc[...] = alpha * acc_sc[...] + _dot(p_.astype(BF16), cb)
    m_sc[...] = m_new

    @pl.when(c == pl.num_programs(1) - 1)
    def _():
        ql = qlat.astype(F32)
        qrf = qr.astype(F32)
        cn = cnew_ref[...]
        kn = krnew_ref[...]
        t_row = lax.broadcasted_iota(jnp.int32, (ql.shape[0], 1), 0) // N_HEADS
        s_new = []
        for j in range(t_new):
            sj = (jnp.sum(ql * cn[j:j + 1, :], axis=-1, keepdims=True)
                  + jnp.sum(qrf * kn[j:j + 1, :], axis=-1, keepdims=True))
            s_new.append(jnp.where(t_row >= j, sj, NEG))
        m_old = m_sc[...]
        m_fin = m_old
        for sj in s_new:
            m_fin = jnp.maximum(m_fin, sj)
        a = jnp.exp(m_old - m_fin)
        l_fin = a * l_sc[...]
        acc = a * acc_sc[...]
        for j, sj in enumerate(s_new):
            pj = jnp.exp(sj - m_fin)
            l_fin = l_fin + pj
            acc = acc + pj * cn[j:j + 1, :]
        o_ref[...] = acc / l_fin


def _paged_attention(qlat, qr, c_new, kr_new, cache_c, cache_kr, page_table, *, pps=16):
    nb, rows, kv_lora = qlat.shape
    n_pages = page_table.shape[1]
    page = cache_c.shape[1]
    t_new = c_new.shape[1]
    pps = min(pps, n_pages)

    def page_spec(width, p):
        return pl.BlockSpec((None, page, width), lambda b, c, pt: (pt[b, c * pps + p], 0, 0))

    per_b = lambda r, w: pl.BlockSpec((None, r, w), lambda b, c, pt: (b, 0, 0))
    in_specs = [per_b(rows, kv_lora), per_b(rows, QK_ROPE), per_b(t_new, kv_lora), per_b(t_new, QK_ROPE)]
    in_specs += [page_spec(kv_lora, p) for p in range(pps)]
    in_specs += [page_spec(QK_ROPE, p) for p in range(pps)]
    grid_spec = pltpu.PrefetchScalarGridSpec(
        num_scalar_prefetch=1, grid=(nb, n_pages // pps), in_specs=in_specs,
        out_specs=per_b(rows, kv_lora),
        scratch_shapes=[pltpu.VMEM((rows, 1), F32), pltpu.VMEM((rows, 1), F32), pltpu.VMEM((rows, kv_lora), F32),
                        pltpu.VMEM((pps * page, kv_lora), BF16), pltpu.VMEM((pps * page, QK_ROPE), BF16)])
    return pl.pallas_call(
        functools.partial(_paged_kernel, pps=pps, page=page, t_new=t_new),
        out_shape=jax.ShapeDtypeStruct((nb, rows, kv_lora), F32), grid_spec=grid_spec, name="paged_sample",
        compiler_params=pltpu.CompilerParams(dimension_semantics=("arbitrary", "arbitrary"),
                                             vmem_limit_bytes=56 * MIB),
    )(page_table, qlat, qr, c_new, kr_new, *([cache_c] * pps), *([cache_kr] * pps))


def _back_kernel(*refs, latent_values):
    if latent_values:
        (h_ref, ycn_ref, ya_ref, wuv_ref, gattn_ref, woc_ref, woa_ref, gffn_ref, wrh_ref, wrl_ref, br_ref,
         hp_ref, xn_ref, gate_ref) = refs
        ya = _dot(ya_ref[...].astype(BF16), wuv_ref[...])
    else:
        (h_ref, ycn_ref, ya_ref, gattn_ref, woc_ref, woa_ref, gffn_ref, wrh_ref, wrl_ref, br_ref,
         hp_ref, xn_ref, gate_ref) = refs
        ya = ya_ref[...]
    yan = _rms(ya, gattn_ref[...]).astype(BF16)
    hp = h_ref[...] + _dot(ycn_ref[...], woc_ref[...]) + _dot(yan, woa_ref[...])
    hp_ref[...] = hp
    xn = _rms(hp, gffn_ref[...])
    xn_ref[...] = xn.astype(BF16)
    x_hi = xn.astype(BF16)
    x_lo = (xn - x_hi.astype(F32)).astype(BF16)
    logits = _dot(x_hi, wrh_ref[...]) + _dot(x_hi, wrl_ref[...]) + _dot(x_lo, wrh_ref[...]) + br_ref[...]
    n_exp = logits.shape[-1]
    lane = lax.broadcasted_iota(jnp.int32, logits.shape, 1)
    work = logits
    picks = []
    for _ in range(TOP_K):
        vmax = jnp.max(work, axis=-1, keepdims=True)
        idx = jnp.min(jnp.where(work == vmax, lane, n_exp), axis=-1, keepdims=True)
        sel = lane == idx
        picks.append((vmax, sel))
        work = jnp.where(sel, -jnp.inf, work)
    top = picks[0][0]
    exps = [jnp.exp(v - top) for v, _ in picks]
    denom = exps[0]
    for e in exps[1:]:
        denom = denom + e
    gates = jnp.zeros_like(logits)
    for e, (_, sel) in zip(exps, picks):
        gates = jnp.where(sel, e / denom, gates)
    gate_ref[...] = gates


def _back(h, ycn, ya, w, *, latent_values, tm=512):
    n, d = h.shape
    tm = min(tm, n)
    n_exp = w["wr_hi"].shape[1]
    row = lambda c: pl.BlockSpec((tm, c), lambda i: (i, 0))
    full = lambda a: pl.BlockSpec(a.shape, lambda i: (0,) * a.ndim)
    shared = [w["g_attn"], w["wo_conv"], w["wo_attn"], w["g_ffn"], w["wr_hi"], w["wr_lo"], w["b_router"]]
    if latent_values:
        shared = [w["w_uv_bd"]] + shared
    return pl.pallas_call(
        functools.partial(_back_kernel, latent_values=latent_values),
        out_shape=[jax.ShapeDtypeStruct((n, d), F32), jax.ShapeDtypeStruct((n, d), BF16),
                   jax.ShapeDtypeStruct((n, n_exp), F32)],
        grid=(n // tm,),
        in_specs=[row(d), row(ycn.shape[1]), row(ya.shape[1])] + [full(a) for a in shared],
        out_specs=[row(d), row(d), row(n_exp)],
        name="back_sample" if latent_values else "back_prompt",
        compiler_params=pltpu.CompilerParams(dimension_semantics=("arbitrary",), vmem_limit_bytes=56 * MIB),
    )(h, ycn, ya, *shared)


def _moe_kernel(xn_ref, gate_ref, hp_ref, wg_ref, wu_ref, bg_ref, bu_ref, wd_ref, bd_ref, gfin_ref, o_ref, acc_sc):
    e = pl.program_id(1)
    f = pl.program_id(2)
    n_e = pl.num_programs(1)
    n_f = pl.num_programs(2)

    @pl.when((e == 0) & (f == 0))
    def _():
        acc_sc[...] = jnp.zeros_like(acc_sc)

    lane = lax.broadcasted_iota(jnp.int32, gate_ref.shape, 1)
    g = jnp.sum(jnp.where(lane == e, gate_ref[...], 0.0), axis=-1, keepdims=True)
    x = xn_ref[...]
    gate = jnp.minimum(_dot(x, wg_ref[...].astype(BF16)) + bg_ref[...], SWIGLU_LIMIT)
    up = jnp.clip(_dot(x, wu_ref[...].astype(BF16)) + bu_ref[...], -SWIGLU_LIMIT, SWIGLU_LIMIT)
    hid = (up + 1.0) * gate * jax.nn.sigmoid(SWIGLU_ALPHA * gate)
    contrib = _dot(hid.astype(BF16), wd_ref[...].astype(BF16))

    @pl.when(f == 0)
    def _():
        acc_sc[...] += g * bd_ref[...]

    acc_sc[...] += g * contrib

    @pl.when((e == n_e - 1) & (f == n_f - 1))
    def _():
        o_ref[...] = _rms(hp_ref[...] + acc_sc[...], gfin_ref[...])


def _moe_dense(xn, gates, hp, w_gu, b_gu, w_d, b_d, g_fin, *, tm=1024, tf=512):
    n, d = xn.shape
    n_exp, _, d_ff2 = w_gu.shape
    d_ff = d_ff2 // 2
    tm = min(tm, n)
    tf = min(tf, d_ff)
    nf = d_ff // tf
    b_gu3 = b_gu.reshape(n_exp, 1, d_ff2)
    b_d3 = b_d.reshape(n_exp, 1, d)
    row = lambda c: pl.BlockSpec((tm, c), lambda i, e, f: (i, 0))
    return pl.pallas_call(
        _moe_kernel,
        out_shape=jax.ShapeDtypeStruct((n, d), F32),
        grid=(n // tm, n_exp, nf),
        in_specs=[row(d), row(n_exp), row(d),
                  pl.BlockSpec((None, d, tf), lambda i, e, f: (e, 0, f)),
                  pl.BlockSpec((None, d, tf), lambda i, e, f: (e, 0, nf + f)),
                  pl.BlockSpec((None, 1, tf), lambda i, e, f: (e, 0, f)),
                  pl.BlockSpec((None, 1, tf), lambda i, e, f: (e, 0, nf + f)),
                  pl.BlockSpec((None, tf, d), lambda i, e, f: (e, f, 0)),
                  pl.BlockSpec((None, 1, d), lambda i, e, f: (e, 0, 0)),
                  pl.BlockSpec((1, d), lambda i, e, f: (0, 0))],
        out_specs=row(d),
        scratch_shapes=[pltpu.VMEM((tm, d), F32)],
        name="moe_dense",
        compiler_params=pltpu.CompilerParams(dimension_semantics=("arbitrary", "arbitrary", "arbitrary"),
                                             vmem_limit_bytes=56 * MIB),
    )(xn, gates, hp, w_gu, w_gu, b_gu3, b_gu3, w_d, b_d3, g_fin)


def _prep_layer(norm_mix_g, w_in, conv_w, q_norm_g, kv_norm_g, w_uq, w_ukv, out_norm_conv_g, out_norm_attn_g,
                w_out, norm_ffn_g, w_router, b_router):
    d = w_in.shape[0]
    d_conv = conv_w.shape[1]
    q_lora = q_norm_g.shape[0]
    kv_lora = kv_norm_g.shape[0]
    c_kr = 3 * d_conv + q_lora + kv_lora
    w_kr = w_in[:, c_kr:c_kr + QK_ROPE]
    w_kr_rot = jnp.concatenate([-w_kr[:, HALF_ROPE:], w_kr[:, :HALF_ROPE]], axis=1)
    w_in_pad = jnp.concatenate([w_in[:, :c_kr], w_kr, w_kr_rot, jnp.zeros((d, LANES - 2 * QK_ROPE), F32)], axis=1)

    qk = QK_NOPE + QK_ROPE
    wq = w_uq.reshape(q_lora, N_HEADS, qk)
    pad = jnp.zeros((q_lora, N_HEADS, LANES - qk), F32)
    wq_pad = jnp.concatenate([wq, pad], axis=2)
    r1 = wq[:, :, QK_NOPE:QK_NOPE + HALF_ROPE]
    r2 = wq[:, :, QK_NOPE + HALF_ROPE:]
    wq_rot = jnp.concatenate([jnp.zeros((q_lora, N_HEADS, QK_NOPE), F32), -r2, r1, pad], axis=2)
    wq2 = jnp.concatenate([wq_pad.reshape(q_lora, -1), wq_rot.reshape(q_lora, -1)], axis=1)

    wkv = w_ukv.reshape(kv_lora, N_HEADS, QK_NOPE + V_DIM)
    w_uk = wkv[:, :, :QK_NOPE]
    w_uv = wkv[:, :, QK_NOPE:]
    wk_pad = jnp.concatenate([w_uk, jnp.zeros((kv_lora, N_HEADS, LANES - QK_NOPE), F32)], axis=2)
    place = jnp.zeros((QK_ROPE, N_HEADS, LANES), F32)
    place = place.at[jnp.arange(QK_ROPE), :, QK_NOPE + jnp.arange(QK_ROPE)].set(1.0)
    wuk_t = jnp.concatenate([jnp.transpose(w_uk, (1, 2, 0)),
                             jnp.zeros((N_HEADS, LANES - QK_NOPE, kv_lora), F32)], axis=1)
    eye = jnp.eye(N_HEADS, dtype=F32)
    w_uv_bd = jnp.einsum("chv,hg->hcgv", w_uv, eye).reshape(N_HEADS * kv_lora, N_HEADS * V_DIM)
    wo_attn = w_out[d_conv:, :]
    wr_hi = w_router.astype(BF16)
    wr_lo = (w_router - wr_hi.astype(F32)).astype(BF16)
    return {
        "d_conv": d_conv, "q_lora": q_lora, "kv_lora": kv_lora,
        "g_mix": norm_mix_g.reshape(1, -1), "w_in": w_in_pad.astype(BF16), "conv_w": conv_w,
        "g_q": q_norm_g.reshape(1, -1), "g_kv": kv_norm_g.reshape(1, -1), "wq2": wq2.astype(BF16),
        "g_conv": out_norm_conv_g.reshape(1, -1), "wk_pad": wk_pad.reshape(kv_lora, -1).astype(BF16),
        "place": place.reshape(QK_ROPE, -1).astype(BF16), "w_uv": w_uv.reshape(kv_lora, -1).astype(BF16),
        "wuk_t": wuk_t.astype(BF16),
        "g_attn": out_norm_attn_g.reshape(1, -1), "wo_conv": w_out[:d_conv, :].astype(BF16),
        "wo_attn": wo_attn.astype(BF16),
        "w_uv_bd": w_uv_bd.astype(BF16),
        "g_ffn": norm_ffn_g.reshape(1, -1), "wr_hi": wr_hi, "wr_lo": wr_lo, "b_router": b_router.reshape(1, -1),
    }


def kernel(x_prompt, x_sample, cache_kv_latent, cache_k_rope, state_conv, page_table, norm_mix_g, w_in, conv_w,
           q_norm_g, kv_norm_g, w_uq, w_ukv, out_norm_conv_g, out_norm_attn_g, w_out, norm_ffn_g, w_router,
           b_router, w_gate_up, b_gate_up, w_down, b_down, final_norm_g):
    depth = w_in.shape[0]
    batch, seq, d = x_prompt.shape
    dec_b, dec_t, _ = x_sample.shape
    past_len = page_table.shape[1] * cache_kv_latent.shape[2]
    assert batch == 1, "the prompt path lays one sequence out along rows"
    pos_p = jnp.arange(seq, dtype=jnp.int32)
    pos_s = jnp.tile(past_len + jnp.arange(dec_t, dtype=jnp.int32), dec_b)
    hp = x_prompt.reshape(seq, d)
    hs = x_sample.reshape(dec_b * dec_t, d)
    g_fin = final_norm_g.reshape(1, -1)
    outs = [[] for _ in range(6)]
    for l in range(depth):
        w = _prep_layer(norm_mix_g[l], w_in[l], conv_w[l], q_norm_g[l], kv_norm_g[l], w_uq[l], w_ukv[l],
                        out_norm_conv_g[l], out_norm_attn_g[l], w_out[l], norm_ffn_g[l], w_router[l], b_router[l])
        d_conv, kv_lora = w["d_conv"], w["kv_lora"]
        last = l == depth - 1
        fin = g_fin if last else jnp.ones_like(g_fin)
        assert last, "deeper stacks need the un-normalised residual between layers"

        u_p, ycn_p, q_p, ckv_p, kr_p, k_p, v_p = _front(hp, pos_p, w, prompt=True)
        ya_p = _flash(q_p, k_p, v_p)
        hp_mid, xn_p, gates_p = _back(hp, ycn_p, ya_p, w, latent_values=False)
        hp = _moe_dense(xn_p, gates_p, hp_mid, w_gate_up[l], b_gate_up[l], w_down[l], b_down[l], fin)

        st = state_conv[l]
        t_idx = jnp.arange(dec_t)
        ov1 = jnp.where((t_idx == 0)[None, :, None], st[:, 1:2, :], 0.0).reshape(dec_b * dec_t, d_conv)
        ov2 = jnp.where((t_idx == 0)[None, :, None], st[:, 0:1, :],
                        jnp.where((t_idx == 1)[None, :, None], st[:, 1:2, :], 0.0)).reshape(dec_b * dec_t, d_conv)
        u_s, ycn_s, q_s, ckv_s, kr_s, qlat_s = _front(hs, pos_s, w, prompt=False, ov=(ov1, ov2), seq_rows=dec_t)
        rows = dec_t * N_HEADS
        qr_s = q_s.reshape(dec_b, rows, LANES)[:, :, QK_NOPE:QK_NOPE + QK_ROPE]
        o_lat = _paged_attention(qlat_s.reshape(dec_b, rows, kv_lora), qr_s,
                                 ckv_s.reshape(dec_b, dec_t, kv_lora), kr_s.reshape(dec_b, dec_t, QK_ROPE),
                                 cache_kv_latent[l], cache_k_rope[l], page_table)
        ya_s = o_lat.reshape(dec_b * dec_t, N_HEADS * kv_lora)
        hs_mid, xn_s, gates_s = _back(hs, ycn_s, ya_s, w, latent_values=True)
        hs = _moe_dense(xn_s, gates_s, hs_mid, w_gate_up[l], b_gate_up[l], w_down[l], b_down[l], fin)

        outs[0].append(ckv_p.reshape(batch, seq, kv_lora))
        outs[1].append(kr_p.reshape(batch, seq, QK_ROPE))
        outs[2].append(u_p[seq - 2:, :].reshape(batch, 2, d_conv))
        outs[3].append(ckv_s.reshape(dec_b, dec_t, kv_lora))
        outs[4].append(kr_s.reshape(dec_b, dec_t, QK_ROPE))
        outs[5].append(u_s.reshape(dec_b, dec_t, d_conv)[:, dec_t - 2:, :])
    return (hp.reshape(batch, seq, d), hs.reshape(dec_b, dec_t, d)) + tuple(jnp.stack(o) for o in outs)
```

```python
import functools
import math

import jax
import jax.numpy as jnp
from jax import lax
from jax.experimental import pallas as pl
from jax.experimental.pallas import tpu as pltpu

N_HEADS = 8
QK_NOPE = 64
QK_ROPE = 32
V_DIM = 64
HALF_ROPE = QK_ROPE // 2
ROPE_THETA = 10000.0
SOFTMAX_SCALE = (QK_NOPE + QK_ROPE) ** -0.5
LOG2E = math.log2(math.e)
TOP_K = 4
SWIGLU_LIMIT = 7.0
SWIGLU_ALPHA = 1.702
RMS_EPS = 1e-6
NEG = -1e30

LANES = 128
MIB = 1024 * 1024
VMEM_LIMIT = 56 * MIB
F32 = jnp.float32
BF16 = jnp.bfloat16

FRONT_ROWS = 512
FLASH_ROWS = 1024
BACK_ROWS = 512
PAGES_PER_CHUNK = 16
MOE_TILE = 512
DISPATCH_ROWS = 512
COMBINE_ROWS = 512
ROUTE_EXPERT, ROUTE_RANK, ROUTE_GATE = 0, TOP_K, 2 * TOP_K

NT_DIMS = (((1,), (1,)), ((), ()))


def _rms(x, g):
    return x * lax.rsqrt(jnp.mean(x * x, axis=-1, keepdims=True) + RMS_EPS) * g


def _dot(a, b):
    return jnp.dot(a, b, preferred_element_type=F32)


def _dot_nt(a, b):
    return lax.dot_general(a, b, NT_DIMS, preferred_element_type=F32)


def _params(n_axes):
    return pltpu.CompilerParams(dimension_semantics=("arbitrary",) * n_axes, vmem_limit_bytes=VMEM_LIMIT)


def _front_kernel(*refs, d_conv, q_lora, kv_lora, seq_rows, prompt):
    if prompt:
        (x_ref, tqc_ref, tqs_ref, tk_ref, gmix_ref, win_ref, cw_ref, gq_ref, gkv_ref, wq_ref, gconv_ref,
         wk_ref, place_ref, wv_ref, vone_ref,
         u_ref, ycn_ref, q_ref, ckv_ref, kr_ref, k_ref, v_ref, carry_ref) = refs
    else:
        (x_ref, tqc_ref, tqs_ref, tk_ref, ov1_ref, ov2_ref, gmix_ref, win_ref, cw_ref, gq_ref, gkv_ref, wq_ref,
         gconv_ref, wukt_ref,
         u_ref, ycn_ref, q_ref, ckv_ref, kr_ref, qlat_ref) = refs
    tm = x_ref.shape[0]
    n = _rms(x_ref[...], gmix_ref[...]).astype(BF16)
    z = _dot(n, win_ref[...])
    c0 = 3 * d_conv
    b_g = z[:, 0:d_conv]
    u = z[:, d_conv:2 * d_conv] * z[:, 2 * d_conv:c0]
    z_q = z[:, c0:c0 + q_lora]
    z_kv = z[:, c0 + q_lora:c0 + q_lora + kv_lora]
    z_r = z[:, c0 + q_lora + kv_lora:]
    u_ref[...] = u

    row = lax.broadcasted_iota(jnp.int32, u.shape, 0)
    r1 = pltpu.roll(u, 1, 0)
    r2 = pltpu.roll(u, 2, 0)
    if prompt:
        @pl.when(pl.program_id(0) == 0)
        def _():
            carry_ref[...] = jnp.zeros_like(carry_ref)
        last1 = carry_ref[7:8, :]
        last2 = carry_ref[6:7, :]
        p1 = jnp.where(row == 0, last1, r1)
        p2 = jnp.where(row == 0, last2, jnp.where(row == 1, last1, r2))
        carry_ref[...] = u[tm - 8:tm, :]
    else:
        t = row % seq_rows
        p1 = jnp.where(t == 0, ov1_ref[...], r1)
        p2 = jnp.where(t < 2, ov2_ref[...], r2)
    cw = cw_ref[...]
    y_c = p2 * cw[0:1, :] + p1 * cw[1:2, :] + u * cw[2:3, :]
    ycn_ref[...] = _rms(b_g * y_c, gconv_ref[...]).astype(BF16)

    nq = _rms(z_q, gq_ref[...]).astype(BF16)
    q2 = _dot(nq, wq_ref[...])
    hw = N_HEADS * LANES
    tqc = tqc_ref[...]
    tqs = tqs_ref[...]
    q_heads = []
    for h in range(N_HEADS):
        qa = q2[:, h * LANES:(h + 1) * LANES]
        qb = q2[:, hw + h * LANES:hw + (h + 1) * LANES]
        q_heads.append((qa * tqc + qb * tqs).astype(BF16))
    q_ref[...] = jnp.concatenate(q_heads, axis=1)

    ckv = _rms(z_kv, gkv_ref[...])
    ckv_ref[...] = ckv
    zr = z_r * tk_ref[...]
    kr = zr[:, 0:QK_ROPE] + zr[:, QK_ROPE:2 * QK_ROPE]
    kr_ref[...] = kr

    if prompt:
        ckv_b = ckv.astype(BF16)
        k_ref[...] = (_dot(ckv_b, wk_ref[...]) + _dot(kr.astype(BF16), place_ref[...])).astype(BF16)
        v_ref[...] = (_dot(ckv_b, wv_ref[...]) + vone_ref[...]).astype(BF16)
    else:
        for h in range(N_HEADS):
            qlat_ref[:, h * kv_lora:(h + 1) * kv_lora] = _dot(q_heads[h], wukt_ref[h]).astype(BF16)


def _front(x, pos, w, *, prompt, ov=None, seq_rows=0):
    n, d = x.shape
    d_conv, q_lora, kv_lora = w["d_conv"], w["q_lora"], w["kv_lora"]
    tm = min(FRONT_ROWS, n)
    hw = N_HEADS * LANES
    inv_freq = ROPE_THETA ** (-jnp.arange(HALF_ROPE, dtype=F32) / HALF_ROPE)
    ang = pos.astype(F32)[:, None] * inv_freq[None, :]
    cos, sin = jnp.cos(ang), jnp.sin(ang)
    zeros = lambda c: jnp.zeros((n, c), F32)
    qs = SOFTMAX_SCALE * LOG2E
    tqc = jnp.concatenate([jnp.full((n, QK_NOPE), qs, F32), cos * qs, cos * qs,
                           zeros(LANES - QK_NOPE - QK_ROPE)], axis=1)
    tqs = jnp.concatenate([zeros(QK_NOPE), sin * qs, sin * qs, zeros(LANES - QK_NOPE - QK_ROPE)], axis=1)
    tk = jnp.concatenate([cos, cos, sin, sin, zeros(LANES - 2 * QK_ROPE)], axis=1)

    row = lambda c: pl.BlockSpec((tm, c), lambda i: (i, 0))
    full = lambda a: pl.BlockSpec(a.shape, lambda i: (0,) * a.ndim)
    ins = [x, tqc, tqs, tk]
    in_specs = [row(d), row(LANES), row(LANES), row(LANES)]
    if not prompt:
        ins += [ov[0], ov[1]]
        in_specs += [row(d_conv), row(d_conv)]
    shared = [w["g_mix"], w["w_in"], w["conv_w"], w["g_q"], w["g_kv"], w["wq2"], w["g_conv"]]
    shared += [w["wk_pad"], w["place"], w["wv_pad"], w["v_one"]] if prompt else [w["wuk_t"]]
    ins += shared
    in_specs += [full(a) for a in shared]

    outs = [jax.ShapeDtypeStruct((n, d_conv), F32), jax.ShapeDtypeStruct((n, d_conv), BF16),
            jax.ShapeDtypeStruct((n, hw), BF16), jax.ShapeDtypeStruct((n, kv_lora), F32),
            jax.ShapeDtypeStruct((n, QK_ROPE), F32)]
    out_specs = [row(d_conv), row(d_conv), row(hw), row(kv_lora), row(QK_ROPE)]
    scratch = []
    if prompt:
        outs += [jax.ShapeDtypeStruct((n, hw), BF16), jax.ShapeDtypeStruct((n, hw), BF16)]
        out_specs += [row(hw), row(hw)]
        scratch = [pltpu.VMEM((8, d_conv), F32)]
    else:
        outs += [jax.ShapeDtypeStruct((n, N_HEADS * kv_lora), BF16)]
        out_specs += [row(N_HEADS * kv_lora)]
    kern = functools.partial(_front_kernel, d_conv=d_conv, q_lora=q_lora, kv_lora=kv_lora, seq_rows=seq_rows,
                             prompt=prompt)
    return pl.pallas_call(
        kern, out_shape=outs, grid=(n // tm,), in_specs=in_specs, out_specs=out_specs, scratch_shapes=scratch,
        name="front_prompt" if prompt else "front_sample", compiler_params=_params(1),
    )(*ins)


def _flash_kernel(q_ref, k_ref, v_ref, o_ref, m_sc, acc_sc, *, tq):
    i = pl.program_id(1)
    m_sc[...] = jnp.full_like(m_sc, NEG)
    acc_sc[...] = jnp.zeros_like(acc_sc)
    reps = tq // LANES

    def step(j, masked):
        start = pl.multiple_of(j * tq, tq)
        for h in range(2):
            sl = slice(h * LANES, (h + 1) * LANES)
            s = _dot_nt(q_ref[:, sl], k_ref[pl.ds(start, tq), sl])
            if masked:
                r = lax.broadcasted_iota(jnp.int32, s.shape, 0)
                c = lax.broadcasted_iota(jnp.int32, s.shape, 1)
                s = jnp.where(c <= r, s, NEG)
            m_prev = m_sc[h]
            m_new = jnp.maximum(m_prev, jnp.max(s, axis=-1, keepdims=True))
            alpha = jnp.exp2(m_prev - m_new)
            p = jnp.exp2(s - jnp.concatenate([m_new] * reps, axis=1))
            acc_sc[h] = alpha * acc_sc[h] + _dot(p.astype(BF16), v_ref[pl.ds(start, tq), sl])
            m_sc[h] = m_new

    def body(j, carry):
        step(j, False)
        return carry

    lax.fori_loop(0, i, body, 0)
    step(i, True)
    outs = []
    for h in range(2):
        a = acc_sc[h]
        outs.append(a[:, :V_DIM] / a[:, V_DIM:V_DIM + 1])
    o_ref[...] = jnp.concatenate(outs, axis=1)


def _flash(q, k, v):
    s = q.shape[0]
    tq = min(FLASH_ROWS, s)
    pairs = N_HEADS // 2
    return pl.pallas_call(
        functools.partial(_flash_kernel, tq=tq),
        out_shape=jax.ShapeDtypeStruct((s, N_HEADS * V_DIM), F32),
        grid=(pairs, s // tq),
        in_specs=[pl.BlockSpec((tq, 2 * LANES), lambda p, i: (i, p)),
                  pl.BlockSpec((s, 2 * LANES), lambda p, i: (0, p)),
                  pl.BlockSpec((s, 2 * LANES), lambda p, i: (0, p))],
        out_specs=pl.BlockSpec((tq, 2 * V_DIM), lambda p, i: (i, p)),
        scratch_shapes=[pltpu.VMEM((2, tq, LANES), F32), pltpu.VMEM((2, tq, LANES), F32)],
        name="flash_prompt", compiler_params=_params(2),
    )(q, k, v)


def _paged_kernel(pt_ref, qlat_ref, qr_ref, cnew_ref, krnew_ref, cache_c, cache_krt, o_ref, cbuf, krbuf, cbf, sem,
                  *, n_pages, page, ppc, t_new):
    b = pl.program_id(0)
    nb = pl.num_programs(0)
    slot = b % 2
    n_chunks = n_pages // ppc
    ck = ppc * page

    def fetch(bb, sl):
        for pg_i in range(n_pages):
            pg = pt_ref[bb, pg_i]
            ci, p = divmod(pg_i, ppc)
            pltpu.make_async_copy(cache_c.at[pg], cbuf.at[sl, pg_i], sem.at[sl]).start()
            pltpu.make_async_copy(cache_krt.at[pg], krbuf.at[sl, ci, :, pl.ds(p * page, page)], sem.at[sl]).start()

    @pl.when(b == 0)
    def _():
        fetch(0, 0)

    @pl.when(b + 1 < nb)
    def _():
        fetch(b + 1, 1 - slot)

    pltpu.make_async_copy(cache_c.at[pl.ds(0, n_pages)], cbuf.at[slot], sem.at[slot]).wait()
    pltpu.make_async_copy(krbuf.at[1 - slot], krbuf.at[slot], sem.at[slot]).wait()

    qlat = qlat_ref[...]
    qr = qr_ref[...]
    rows, kv_lora = qlat.shape

    def scores(ci):
        cb = cbuf[slot, pl.ds(ci * ppc, ppc)].reshape(ck, kv_lora).astype(BF16)
        cbf[ci % 2] = cb
        return _dot_nt(qlat, cb) + _dot(qr, krbuf[slot, ci].astype(BF16))

    def absorb(ci, s, m_prev, l_prev, acc):
        m_new = jnp.maximum(m_prev, jnp.max(s, axis=-1, keepdims=True))
        alpha = jnp.exp2(m_prev - m_new)
        p_ = jnp.exp2(s - m_new)
        l_new = alpha * l_prev + jnp.sum(p_, axis=-1, keepdims=True)
        return m_new, l_new, alpha * acc + _dot(p_.astype(BF16), cbf[ci % 2])

    def chunk(ci, carry):
        s, m_prev, l_prev, acc = carry
        s_next = scores(ci + 1)
        return (s_next,) + absorb(ci, s, m_prev, l_prev, acc)

    init = (scores(0), jnp.full((rows, 1), NEG, F32), jnp.zeros((rows, 1), F32), jnp.zeros((rows, kv_lora), F32))
    s_last, m_old, l_old, acc = lax.fori_loop(0, n_chunks - 1, chunk, init)
    m_old, l_old, acc = absorb(n_chunks - 1, s_last, m_old, l_old, acc)

    ql = qlat.astype(F32)
    qrf = qr.astype(F32)
    cn = cnew_ref[...]
    kn = krnew_ref[...]
    t_row = lax.broadcasted_iota(jnp.int32, (rows, 1), 0) // N_HEADS
    s_new = []
    for j in range(t_new):
        sj = (jnp.sum(ql * cn[j:j + 1, :], axis=-1, keepdims=True)
              + jnp.sum(qrf * kn[j:j + 1, :], axis=-1, keepdims=True))
        s_new.append(jnp.where(t_row >= j, sj, NEG))
    m_fin = m_old
    for sj in s_new:
        m_fin = jnp.maximum(m_fin, sj)
    a = jnp.exp2(m_old - m_fin)
    l_fin = a * l_old
    acc = a * acc
    for j, sj in enumerate(s_new):
        pj = jnp.exp2(sj - m_fin)
        l_fin = l_fin + pj
        acc = acc + pj * cn[j:j + 1, :]
    o_ref[...] = acc / l_fin


def _paged_attention(qlat, qr, c_new, kr_new, cache_c, cache_krt, page_table):
    nb, rows, kv_lora = qlat.shape
    n_pages = page_table.shape[1]
    page = cache_c.shape[1]
    t_new = c_new.shape[1]
    ppc = math.gcd(PAGES_PER_CHUNK, n_pages)
    per_b = lambda r, w: pl.BlockSpec((None, r, w), lambda b, pt: (b, 0, 0))
    any_spec = pl.BlockSpec(memory_space=pl.ANY)
    grid_spec = pltpu.PrefetchScalarGridSpec(
        num_scalar_prefetch=1, grid=(nb,),
        in_specs=[per_b(rows, kv_lora), per_b(rows, QK_ROPE), per_b(t_new, kv_lora), per_b(t_new, QK_ROPE),
                  any_spec, any_spec],
        out_specs=per_b(rows, kv_lora),
        scratch_shapes=[pltpu.VMEM((2, n_pages, page, kv_lora), F32),
                        pltpu.VMEM((2, n_pages // ppc, QK_ROPE, ppc * page), F32),
                        pltpu.VMEM((2, ppc * page, kv_lora), BF16),
                        pltpu.SemaphoreType.DMA((2,))])
    return pl.pallas_call(
        functools.partial(_paged_kernel, n_pages=n_pages, page=page, ppc=ppc, t_new=t_new),
        out_shape=jax.ShapeDtypeStruct((nb, rows, kv_lora), F32), grid_spec=grid_spec, name="paged_sample",
        compiler_params=_params(1),
    )(page_table, qlat, qr, c_new, kr_new, cache_c, cache_krt)


def _back_kernel(*refs, latent_values):
    if latent_values:
        (h_ref, ycn_ref, ya_ref, wuv_ref, gattn_ref, woc_ref, woa_ref, gffn_ref, wrh_ref, wrl_ref, br_ref,
         hp_ref, xn_ref, route_ref, count_ref, base_sc) = refs
        ya = _dot(ya_ref[...].astype(BF16), wuv_ref[...])
    else:
        (h_ref, ycn_ref, ya_ref, gattn_ref, woc_ref, woa_ref, gffn_ref, wrh_ref, wrl_ref, br_ref,
         hp_ref, xn_ref, route_ref, count_ref, base_sc) = refs
        ya = ya_ref[...]
    yan = _rms(ya, gattn_ref[...]).astype(BF16)
    hp = h_ref[...] + _dot(ycn_ref[...], woc_ref[...]) + _dot(yan, woa_ref[...])
    hp_ref[...] = hp
    xn = _rms(hp, gffn_ref[...])
    xn_ref[...] = xn
    x_hi = xn.astype(BF16)
    x_lo = (xn - x_hi.astype(F32)).astype(BF16)
    logits = _dot(x_hi, wrh_ref[...]) + _dot(x_hi, wrl_ref[...]) + _dot(x_lo, wrh_ref[...]) + br_ref[...]
    tm, n_exp = logits.shape
    lane = lax.broadcasted_iota(jnp.int32, logits.shape, 1)
    work = logits
    picks = []
    for _ in range(TOP_K):
        vmax = jnp.max(work, axis=-1, keepdims=True)
        idx = jnp.min(jnp.where(work == vmax, lane, n_exp), axis=-1, keepdims=True)
        sel = lane == idx
        picks.append((vmax, idx, sel))
        work = jnp.where(sel, -jnp.inf, work)
    top = picks[0][0]
    exps = [jnp.exp(v - top) for v, _, _ in picks]
    denom = exps[0]
    for e in exps[1:]:
        denom = denom + e

    @pl.when(pl.program_id(0) == 0)
    def _():
        base_sc[...] = jnp.zeros_like(base_sc)
    onehot = jnp.zeros_like(logits)
    for _, _, sel in picks:
        onehot = jnp.where(sel, 1.0, onehot)
    r_i = lax.broadcasted_iota(jnp.int32, (tm, tm), 0)
    c_i = lax.broadcasted_iota(jnp.int32, (tm, tm), 1)
    earlier = jnp.where(c_i < r_i, 1.0, 0.0).astype(BF16)
    rank_all = _dot(earlier, onehot.astype(BF16)) + base_sc[...]
    base_sc[...] += jnp.sum(onehot, axis=0, keepdims=True)
    count_ref[...] = base_sc[...]

    lane_r = lax.broadcasted_iota(jnp.int32, (tm, LANES), 1)
    route = jnp.zeros((tm, LANES), F32)
    for k, (e, (_, idx, sel)) in enumerate(zip(exps, picks)):
        rank_k = jnp.sum(jnp.where(sel, rank_all, 0.0), axis=-1, keepdims=True)
        route = jnp.where(lane_r == ROUTE_EXPERT + k, idx.astype(F32), route)
        route = jnp.where(lane_r == ROUTE_RANK + k, rank_k, route)
        route = jnp.where(lane_r == ROUTE_GATE + k, e / denom, route)
    route_ref[...] = route


def _back(h, ycn, ya, w, *, latent_values):
    n, d = h.shape
    tm = min(BACK_ROWS, n)
    n_exp = w["wr_hi"].shape[1]
    row = lambda c: pl.BlockSpec((tm, c), lambda i: (i, 0))
    full = lambda a: pl.BlockSpec(a.shape, lambda i: (0,) * a.ndim)
    shared = [w["g_attn"], w["wo_conv"], w["wo_attn"], w["g_ffn"], w["wr_hi"], w["wr_lo"], w["b_router"]]
    if latent_values:
        shared = [w["w_uv_bd"]] + shared
    return pl.pallas_call(
        functools.partial(_back_kernel, latent_values=latent_values),
        out_shape=[jax.ShapeDtypeStruct((n, d), F32), jax.ShapeDtypeStruct((n, d), F32),
                   jax.ShapeDtypeStruct((n, LANES), F32), jax.ShapeDtypeStruct((1, n_exp), F32)],
        grid=(n // tm,),
        in_specs=[row(d), row(ycn.shape[1]), row(ya.shape[1])] + [full(a) for a in shared],
        out_specs=[row(d), row(d), row(LANES), pl.BlockSpec((1, n_exp), lambda i: (0, 0))],
        scratch_shapes=[pltpu.VMEM((1, n_exp), F32)],
        name="back_sample" if latent_values else "back_prompt", compiler_params=_params(1),
    )(h, ycn, ya, *shared)


def _dispatch_kernel(slot_ref, x_ref, xs_in_ref, xs_ref, sem):
    del xs_in_ref
    tm = x_ref.shape[0]

    def body(r, carry):
        for k in range(TOP_K):
            s = slot_ref[r * TOP_K + k]
            pltpu.make_async_copy(x_ref.at[pl.ds(r, 1)], xs_ref.at[pl.ds(s, 1)], sem).start()
        return carry

    lax.fori_loop(0, tm, body, 0, unroll=4)
    for k in range(TOP_K):
        pltpu.make_async_copy(x_ref, xs_ref.at[pl.ds(0, tm)], sem).wait()


def _dispatch(x, slots, xs):
    n, d = x.shape
    tm = min(DISPATCH_ROWS, n)
    return pl.pallas_call(
        _dispatch_kernel,
        out_shape=jax.ShapeDtypeStruct(xs.shape, F32),
        grid=(n // tm,),
        in_specs=[pl.BlockSpec((tm * TOP_K,), lambda i: (i,), memory_space=pltpu.SMEM),
                  pl.BlockSpec((tm, d), lambda i: (i, 0)),
                  pl.BlockSpec(memory_space=pl.ANY)],
        out_specs=pl.BlockSpec(memory_space=pl.ANY),
        scratch_shapes=[pltpu.SemaphoreType.DMA(())],
        input_output_aliases={2: 0},
        name="moe_dispatch", compiler_params=_params(1),
    )(slots, x, xs)


def _expert_kernel(te_ref, nu_ref, xs_ref, wgu_ref, bgu_ref, wd_ref, bd_ref, ys_ref, wgu_bf, wd_bf):
    ti = pl.program_id(0)
    used = ti < nu_ref[0]
    new_expert = (ti == 0) | (te_ref[ti] != te_ref[jnp.maximum(ti - 1, 0)])
    d_ff = wd_ref.shape[0]

    @pl.when(used & new_expert)
    def _():
        wgu_bf[...] = wgu_ref[...].astype(BF16)
        wd_bf[...] = wd_ref[...].astype(BF16)

    @pl.when(used)
    def _():
        x = xs_ref[...].astype(BF16)
        bgu = bgu_ref[...]
        gate = jnp.minimum(_dot(x, wgu_bf[:, :d_ff]) + bgu[:, :d_ff], SWIGLU_LIMIT)
        up = jnp.clip(_dot(x, wgu_bf[:, d_ff:]) + bgu[:, d_ff:], -SWIGLU_LIMIT, SWIGLU_LIMIT)
        hid = (up + 1.0) * gate * jax.nn.sigmoid(SWIGLU_ALPHA * gate)
        ys_ref[...] = _dot(hid.astype(BF16), wd_bf[...]) + bd_ref[...]

    @pl.when(jnp.logical_not(used))
    def _():
        ys_ref[...] = jnp.zeros_like(ys_ref)


def _experts(xs, tile_expert, n_used, w_gu, b_gu, w_d, b_d):
    n_slots, d = xs.shape
    n_exp, _, d_ff2 = w_gu.shape
    d_ff = d_ff2 // 2
    tmm = MOE_TILE
    grid_spec = pltpu.PrefetchScalarGridSpec(
        num_scalar_prefetch=2, grid=(n_slots // tmm,),
        in_specs=[pl.BlockSpec((tmm, d), lambda t, te, nu: (t, 0)),
                  pl.BlockSpec((None, d, d_ff2), lambda t, te, nu: (te[t], 0, 0)),
                  pl.BlockSpec((None, 1, d_ff2), lambda t, te, nu: (te[t], 0, 0)),
                  pl.BlockSpec((None, d_ff, d), lambda t, te, nu: (te[t], 0, 0)),
                  pl.BlockSpec((None, 1, d), lambda t, te, nu: (te[t], 0, 0))],
        out_specs=pl.BlockSpec((tmm, d), lambda t, te, nu: (t, 0)),
        scratch_shapes=[pltpu.VMEM((d, d_ff2), BF16), pltpu.VMEM((d_ff, d), BF16)])
    return pl.pallas_call(
        _expert_kernel, out_shape=jax.ShapeDtypeStruct((n_slots, d), F32), grid_spec=grid_spec,
        name="moe_experts", compiler_params=_params(1),
    )(tile_expert, n_used, xs, w_gu, b_gu.reshape(n_exp, 1, d_ff2), w_d, b_d.reshape(n_exp, 1, d))


def _combine_kernel(slot_ref, route_ref, hp_ref, ys_ref, gfin_ref, o_ref, buf, sem):
    tm = hp_ref.shape[0]

    def body(r, carry):
        for k in range(TOP_K):
            s = slot_ref[r * TOP_K + k]
            pltpu.make_async_copy(ys_ref.at[pl.ds(s, 1)], buf.at[k, pl.ds(r, 1)], sem).start()
        return carry

    lax.fori_loop(0, tm, body, 0, unroll=4)
    for k in range(TOP_K):
        pltpu.make_async_copy(ys_ref.at[pl.ds(0, tm)], buf.at[k], sem).wait()
    route = route_ref[...]
    acc = hp_ref[...]
    for k in range(TOP_K):
        acc = acc + route[:, ROUTE_GATE + k:ROUTE_GATE + k + 1] * buf[k]
    o_ref[...] = _rms(acc, gfin_ref[...])


def _combine(slots, route, hp, ys, g_fin):
    n, d = hp.shape
    tm = min(COMBINE_ROWS, n)
    return pl.pallas_call(
        _combine_kernel,
        out_shape=jax.ShapeDtypeStruct((n, d), F32),
        grid=(n // tm,),
        in_specs=[pl.BlockSpec((tm * TOP_K,), lambda i: (i,), memory_space=pltpu.SMEM),
                  pl.BlockSpec((tm, LANES), lambda i: (i, 0)),
                  pl.BlockSpec((tm, d), lambda i: (i, 0)),
                  pl.BlockSpec(memory_space=pl.ANY),
                  pl.BlockSpec((1, d), lambda i: (0, 0))],
        out_specs=pl.BlockSpec((tm, d), lambda i: (i, 0)),
        scratch_shapes=[pltpu.VMEM((TOP_K, tm, d), F32), pltpu.SemaphoreType.DMA(())],
        name="moe_combine", compiler_params=_params(1),
    )(slots, route, hp, ys, g_fin)


def _slot_tables(routes, counts, n_exp):
    tmm = MOE_TILE
    counts = [c.reshape(n_exp).astype(jnp.int32) for c in counts]
    total = sum(counts)
    group = ((total + tmm - 1) // tmm) * tmm
    ends = jnp.cumsum(group)
    start = ends - group
    slots = []
    before = jnp.zeros_like(total)
    for route, cnt in zip(routes, counts):
        eid = route[:, ROUTE_EXPERT:ROUTE_EXPERT + TOP_K].astype(jnp.int32)
        rank = route[:, ROUTE_RANK:ROUTE_RANK + TOP_K].astype(jnp.int32)
        slots.append(((start + before)[eid] + rank).reshape(-1))
        before = before + cnt
    n_tokens = sum(r.shape[0] for r in routes)
    max_tiles = (n_tokens * TOP_K) // tmm + n_exp
    n_used = ends[-1] // tmm
    tile_start = jnp.arange(max_tiles, dtype=jnp.int32) * tmm
    tile_expert = jnp.searchsorted(ends, jnp.minimum(tile_start, ends[-1] - 1), side="right").astype(jnp.int32)
    return slots, tile_expert, n_used.reshape(1).astype(jnp.int32), max_tiles * tmm


def _prep_layer(norm_mix_g, w_in, conv_w, q_norm_g, kv_norm_g, w_uq, w_ukv, out_norm_conv_g, out_norm_attn_g,
                w_out, norm_ffn_g, w_router, b_router):
    d = w_in.shape[0]
    d_conv = conv_w.shape[1]
    q_lora = q_norm_g.shape[0]
    kv_lora = kv_norm_g.shape[0]
    c_kr = 3 * d_conv + q_lora + kv_lora
    w_kr = w_in[:, c_kr:c_kr + QK_ROPE]
    w_kr_rot = jnp.concatenate([-w_kr[:, HALF_ROPE:], w_kr[:, :HALF_ROPE]], axis=1)
    w_in_pad = jnp.concatenate([w_in[:, :c_kr], w_kr, w_kr_rot, jnp.zeros((d, LANES - 2 * QK_ROPE), F32)], axis=1)

    qk = QK_NOPE + QK_ROPE
    wq = w_uq.reshape(q_lora, N_HEADS, qk)
    pad = jnp.zeros((q_lora, N_HEADS, LANES - qk), F32)
    wq_pad = jnp.concatenate([wq, pad], axis=2)
    r1 = wq[:, :, QK_NOPE:QK_NOPE + HALF_ROPE]
    r2 = wq[:, :, QK_NOPE + HALF_ROPE:]
    wq_rot = jnp.concatenate([jnp.zeros((q_lora, N_HEADS, QK_NOPE), F32), -r2, r1, pad], axis=2)
    wq2 = jnp.concatenate([wq_pad.reshape(q_lora, -1), wq_rot.reshape(q_lora, -1)], axis=1)

    wkv = w_ukv.reshape(kv_lora, N_HEADS, QK_NOPE + V_DIM)
    w_uk = wkv[:, :, :QK_NOPE]
    w_uv = wkv[:, :, QK_NOPE:]
    wk_pad = jnp.concatenate([w_uk, jnp.zeros((kv_lora, N_HEADS, LANES - QK_NOPE), F32)], axis=2)
    wv_pad = jnp.concatenate([w_uv, jnp.zeros((kv_lora, N_HEADS, LANES - V_DIM), F32)], axis=2)
    v_one = jnp.zeros((1, N_HEADS, LANES), F32).at[:, :, V_DIM].set(1.0)
    place = jnp.zeros((QK_ROPE, N_HEADS, LANES), F32)
    place = place.at[jnp.arange(QK_ROPE), :, QK_NOPE + jnp.arange(QK_ROPE)].set(1.0)
    wuk_t = jnp.concatenate([jnp.transpose(w_uk, (1, 2, 0)),
                             jnp.zeros((N_HEADS, LANES - QK_NOPE, kv_lora), F32)], axis=1)
    eye = jnp.eye(N_HEADS, dtype=F32)
    w_uv_bd = jnp.einsum("chv,hg->hcgv", w_uv, eye).reshape(N_HEADS * kv_lora, N_HEADS * V_DIM)
    wr_hi = w_router.astype(BF16)
    wr_lo = (w_router - wr_hi.astype(F32)).astype(BF16)
    return {
        "d_conv": d_conv, "q_lora": q_lora, "kv_lora": kv_lora,
        "g_mix": norm_mix_g.reshape(1, -1), "w_in": w_in_pad.astype(BF16), "conv_w": conv_w,
        "g_q": q_norm_g.reshape(1, -1), "g_kv": kv_norm_g.reshape(1, -1), "wq2": wq2.astype(BF16),
        "g_conv": out_norm_conv_g.reshape(1, -1), "wk_pad": wk_pad.reshape(kv_lora, -1).astype(BF16),
        "place": place.reshape(QK_ROPE, -1).astype(BF16), "wv_pad": wv_pad.reshape(kv_lora, -1).astype(BF16),
        "v_one": v_one.reshape(1, -1), "wuk_t": wuk_t.astype(BF16),
        "g_attn": out_norm_attn_g.reshape(1, -1), "wo_conv": w_out[:d_conv, :].astype(BF16),
        "wo_attn": w_out[d_conv:, :].astype(BF16), "w_uv_bd": w_uv_bd.astype(BF16),
        "g_ffn": norm_ffn_g.reshape(1, -1), "wr_hi": wr_hi, "wr_lo": wr_lo, "b_router": b_router.reshape(1, -1),
    }


def kernel(x_prompt, x_sample, cache_kv_latent, cache_k_rope, state_conv, page_table, norm_mix_g, w_in, conv_w,
           q_norm_g, kv_norm_g, w_uq, w_ukv, out_norm_conv_g, out_norm_attn_g, w_out, norm_ffn_g, w_router,
           b_router, w_gate_up, b_gate_up, w_down, b_down, final_norm_g):
    depth = w_in.shape[0]
    batch, seq, d = x_prompt.shape
    dec_b, dec_t, _ = x_sample.shape
    n_exp = w_router.shape[-1]
    past_len = page_table.shape[1] * cache_kv_latent.shape[2]
    assert batch == 1, "the prompt path lays one sequence out along rows"
    assert depth == 1, "deeper stacks need the un-normalised residual between layers"
    pos_p = jnp.arange(seq, dtype=jnp.int32)
    pos_s = jnp.tile(past_len + jnp.arange(dec_t, dtype=jnp.int32), dec_b)
    hp = x_prompt.reshape(seq, d)
    hs = x_sample.reshape(dec_b * dec_t, d)
    g_fin = final_norm_g.reshape(1, -1)
    outs = [[] for _ in range(6)]
    for l in range(depth):
        w = _prep_layer(norm_mix_g[l], w_in[l], conv_w[l], q_norm_g[l], kv_norm_g[l], w_uq[l], w_ukv[l],
                        out_norm_conv_g[l], out_norm_attn_g[l], w_out[l], norm_ffn_g[l], w_router[l], b_router[l])
        d_conv, kv_lora = w["d_conv"], w["kv_lora"]

        u_p, ycn_p, q_p, ckv_p, kr_p, k_p, v_p = _front(hp, pos_p, w, prompt=True)
        ya_p = _flash(q_p, k_p, v_p)
        hp_mid, xn_p, route_p, count_p = _back(hp, ycn_p, ya_p, w, latent_values=False)

        st = state_conv[l]
        t_idx = jnp.arange(dec_t)
        ov1 = jnp.where((t_idx == 0)[None, :, None], st[:, 1:2, :], 0.0).reshape(dec_b * dec_t, d_conv)
        ov2 = jnp.where((t_idx == 0)[None, :, None], st[:, 0:1, :],
                        jnp.where((t_idx == 1)[None, :, None], st[:, 1:2, :], 0.0)).reshape(dec_b * dec_t, d_conv)
        u_s, ycn_s, q_s, ckv_s, kr_s, qlat_s = _front(hs, pos_s, w, prompt=False, ov=(ov1, ov2), seq_rows=dec_t)
        rows = dec_t * N_HEADS
        qr_s = q_s.reshape(dec_b, rows, LANES)[:, :, QK_NOPE:QK_NOPE + QK_ROPE]
        cache_krt = jnp.swapaxes(cache_k_rope[l], 1, 2)
        o_lat = _paged_attention(qlat_s.reshape(dec_b, rows, kv_lora), qr_s,
                                 ckv_s.reshape(dec_b, dec_t, kv_lora), kr_s.reshape(dec_b, dec_t, QK_ROPE),
                                 cache_kv_latent[l], cache_krt, page_table)
        ya_s = o_lat.reshape(dec_b * dec_t, N_HEADS * kv_lora)
        hs_mid, xn_s, route_s, count_s = _back(hs, ycn_s, ya_s, w, latent_values=True)

        (slot_p, slot_s), tile_expert, n_used, n_slots = _slot_tables((route_p, route_s), (count_p, count_s), n_exp)
        xs = jnp.zeros((n_slots, d), F32)
        xs = _dispatch(xn_p, slot_p, xs)
        xs = _dispatch(xn_s, slot_s, xs)
        ys = _experts(xs, tile_expert, n_used, w_gate_up[l], b_gate_up[l], w_down[l], b_down[l])
        hp = _combine(slot_p, route_p, hp_mid, ys, g_fin)
        hs = _combine(slot_s, route_s, hs_mid, ys, g_fin)

        outs[0].append(ckv_p.reshape(batch, seq, kv_lora))
        outs[1].append(kr_p.reshape(batch, seq, QK_ROPE))
        outs[2].append(u_p[seq - 2:, :].reshape(batch, 2, d_conv))
        outs[3].append(ckv_s.reshape(dec_b, dec_t, kv_lora))
        outs[4].append(kr_s.reshape(dec_b, dec_t, QK_ROPE))
        outs[5].append(u_s.reshape(dec_b, dec_t, d_conv)[:, dec_t - 2:, :])
    return (hp.reshape(batch, seq, d), hs.reshape(dec_b, dec_t, d)) + tuple(jnp.stack(o) for o in outs)
```

```python
import functools
import math

import jax
import jax.numpy as jnp
from jax import lax
from jax.experimental import pallas as pl
from jax.experimental.pallas import tpu as pltpu

N_HEADS = 8
QK_NOPE = 64
QK_ROPE = 32
V_DIM = 64
HALF_ROPE = QK_ROPE // 2
ROPE_THETA = 10000.0
SOFTMAX_SCALE = (QK_NOPE + QK_ROPE) ** -0.5
LOG2E = math.log2(math.e)
TOP_K = 4
SWIGLU_LIMIT = 7.0
SWIGLU_ALPHA = 1.702
RMS_EPS = 1e-6
NEG = -1e30

LANES = 128
MIB = 1024 * 1024
VMEM_LIMIT = 56 * MIB
F32 = jnp.float32
BF16 = jnp.bfloat16

FRONT_ROWS = 512
FLASH_ROWS = 1024
BACK_ROWS = 512
PAGES_PER_CHUNK = 16
MOE_TILE = 512
DISPATCH_ROWS = 512
COMBINE_ROWS = 512
ROUTE_EXPERT, ROUTE_RANK, ROUTE_GATE = 0, TOP_K, 2 * TOP_K

NT_DIMS = (((1,), (1,)), ((), ()))


def _rms(x, g):
    return x * lax.rsqrt(jnp.mean(x * x, axis=-1, keepdims=True) + RMS_EPS) * g


def _dot(a, b):
    return jnp.dot(a, b, preferred_element_type=F32)


def _dot_nt(a, b):
    return lax.dot_general(a, b, NT_DIMS, preferred_element_type=F32)


def _params(n_axes):
    return pltpu.CompilerParams(dimension_semantics=("arbitrary",) * n_axes, vmem_limit_bytes=VMEM_LIMIT)


def _front_kernel(*refs, d_conv, q_lora, kv_lora, seq_rows, prompt):
    if prompt:
        (x_ref, tqc_ref, tqs_ref, tk_ref, gmix_ref, win_ref, cw_ref, gq_ref, gkv_ref, wq_ref, gconv_ref,
         wk_ref, place_ref, wv_ref, vone_ref,
         u_ref, ycn_ref, q_ref, ckv_ref, kr_ref, k_ref, v_ref, carry_ref) = refs
    else:
        (x_ref, tqc_ref, tqs_ref, tk_ref, ov1_ref, ov2_ref, gmix_ref, win_ref, cw_ref, gq_ref, gkv_ref, wq_ref,
         gconv_ref, wukt_ref,
         u_ref, ycn_ref, q_ref, ckv_ref, kr_ref, qlat_ref) = refs
    tm = x_ref.shape[0]
    n = _rms(x_ref[...], gmix_ref[...]).astype(BF16)
    z = _dot(n, win_ref[...])
    c0 = 3 * d_conv
    b_g = z[:, 0:d_conv]
    u = z[:, d_conv:2 * d_conv] * z[:, 2 * d_conv:c0]
    z_q = z[:, c0:c0 + q_lora]
    z_kv = z[:, c0 + q_lora:c0 + q_lora + kv_lora]
    z_r = z[:, c0 + q_lora + kv_lora:]
    u_ref[...] = u

    row = lax.broadcasted_iota(jnp.int32, u.shape, 0)
    r1 = pltpu.roll(u, 1, 0)
    r2 = pltpu.roll(u, 2, 0)
    if prompt:
        @pl.when(pl.program_id(0) == 0)
        def _():
            carry_ref[...] = jnp.zeros_like(carry_ref)
        last1 = carry_ref[7:8, :]
        last2 = carry_ref[6:7, :]
        p1 = jnp.where(row == 0, last1, r1)
        p2 = jnp.where(row == 0, last2, jnp.where(row == 1, last1, r2))
        carry_ref[...] = u[tm - 8:tm, :]
    else:
        t = row % seq_rows
        p1 = jnp.where(t == 0, ov1_ref[...], r1)
        p2 = jnp.where(t < 2, ov2_ref[...], r2)
    cw = cw_ref[...]
    y_c = p2 * cw[0:1, :] + p1 * cw[1:2, :] + u * cw[2:3, :]
    ycn_ref[...] = _rms(b_g * y_c, gconv_ref[...]).astype(BF16)

    nq = _rms(z_q, gq_ref[...]).astype(BF16)
    q2 = _dot(nq, wq_ref[...])
    hw = N_HEADS * LANES
    tqc = tqc_ref[...]
    tqs = tqs_ref[...]
    q_heads = []
    for h in range(N_HEADS):
        qa = q2[:, h * LANES:(h + 1) * LANES]
        qb = q2[:, hw + h * LANES:hw + (h + 1) * LANES]
        q_heads.append((qa * tqc + qb * tqs).astype(BF16))
    q_ref[...] = jnp.concatenate(q_heads, axis=1)

    ckv = _rms(z_kv, gkv_ref[...])
    ckv_ref[...] = ckv
    zr = z_r * tk_ref[...]
    kr = zr[:, 0:QK_ROPE] + zr[:, QK_ROPE:2 * QK_ROPE]
    kr_ref[...] = kr

    if prompt:
        ckv_b = ckv.astype(BF16)
        k_ref[...] = (_dot(ckv_b, wk_ref[...]) + _dot(kr.astype(BF16), place_ref[...])).astype(BF16)
        v_ref[...] = (_dot(ckv_b, wv_ref[...]) + vone_ref[...]).astype(BF16)
    else:
        for h in range(N_HEADS):
            qlat_ref[:, h * kv_lora:(h + 1) * kv_lora] = _dot(q_heads[h], wukt_ref[h]).astype(BF16)


def _front(x, pos, w, *, prompt, ov=None, seq_rows=0):
    n, d = x.shape
    d_conv, q_lora, kv_lora = w["d_conv"], w["q_lora"], w["kv_lora"]
    tm = min(FRONT_ROWS, n)
    hw = N_HEADS * LANES
    inv_freq = ROPE_THETA ** (-jnp.arange(HALF_ROPE, dtype=F32) / HALF_ROPE)
    ang = pos.astype(F32)[:, None] * inv_freq[None, :]
    cos, sin = jnp.cos(ang), jnp.sin(ang)
    zeros = lambda c: jnp.zeros((n, c), F32)
    qs = SOFTMAX_SCALE * LOG2E
    tqc = jnp.concatenate([jnp.full((n, QK_NOPE), qs, F32), cos * qs, cos * qs,
                           zeros(LANES - QK_NOPE - QK_ROPE)], axis=1)
    tqs = jnp.concatenate([zeros(QK_NOPE), sin * qs, sin * qs, zeros(LANES - QK_NOPE - QK_ROPE)], axis=1)
    tk = jnp.concatenate([cos, cos, sin, sin, zeros(LANES - 2 * QK_ROPE)], axis=1)

    row = lambda c: pl.BlockSpec((tm, c), lambda i: (i, 0))
    full = lambda a: pl.BlockSpec(a.shape, lambda i: (0,) * a.ndim)
    ins = [x, tqc, tqs, tk]
    in_specs = [row(d), row(LANES), row(LANES), row(LANES)]
    if not prompt:
        ins += [ov[0], ov[1]]
        in_specs += [row(d_conv), row(d_conv)]
    shared = [w["g_mix"], w["w_in"], w["conv_w"], w["g_q"], w["g_kv"], w["wq2"], w["g_conv"]]
    shared += [w["wk_pad"], w["place"], w["wv_pad"], w["v_one"]] if prompt else [w["wuk_t"]]
    ins += shared
    in_specs += [full(a) for a in shared]

    outs = [jax.ShapeDtypeStruct((n, d_conv), F32), jax.ShapeDtypeStruct((n, d_conv), BF16),
            jax.ShapeDtypeStruct((n, hw), BF16), jax.ShapeDtypeStruct((n, kv_lora), F32),
            jax.ShapeDtypeStruct((n, QK_ROPE), F32)]
    out_specs = [row(d_conv), row(d_conv), row(hw), row(kv_lora), row(QK_ROPE)]
    scratch = []
    if prompt:
        outs += [jax.ShapeDtypeStruct((n, hw), BF16), jax.ShapeDtypeStruct((n, hw), BF16)]
        out_specs += [row(hw), row(hw)]
        scratch = [pltpu.VMEM((8, d_conv), F32)]
    else:
        outs += [jax.ShapeDtypeStruct((n, N_HEADS * kv_lora), BF16)]
        out_specs += [row(N_HEADS * kv_lora)]
    kern = functools.partial(_front_kernel, d_conv=d_conv, q_lora=q_lora, kv_lora=kv_lora, seq_rows=seq_rows,
                             prompt=prompt)
    return pl.pallas_call(
        kern, out_shape=outs, grid=(n // tm,), in_specs=in_specs, out_specs=out_specs, scratch_shapes=scratch,
        name="front_prompt" if prompt else "front_sample", compiler_params=_params(1),
    )(*ins)


def _flash_kernel(q_ref, k_ref, v_ref, o_ref, m_sc, acc_sc, *, tq):
    i = pl.program_id(1)
    m_sc[...] = jnp.full_like(m_sc, NEG)
    acc_sc[...] = jnp.zeros_like(acc_sc)
    reps = tq // LANES

    def step(j, masked):
        start = pl.multiple_of(j * tq, tq)
        for h in range(2):
            sl = slice(h * LANES, (h + 1) * LANES)
            s = _dot_nt(q_ref[:, sl], k_ref[pl.ds(start, tq), sl])
            if masked:
                r = lax.broadcasted_iota(jnp.int32, s.shape, 0)
                c = lax.broadcasted_iota(jnp.int32, s.shape, 1)
                s = jnp.where(c <= r, s, NEG)
            m_prev = m_sc[h]
            m_new = jnp.maximum(m_prev, jnp.max(s, axis=-1, keepdims=True))
            alpha = jnp.exp2(m_prev - m_new)
            p = jnp.exp2(s - jnp.concatenate([m_new] * reps, axis=1))
            acc_sc[h] = alpha * acc_sc[h] + _dot(p.astype(BF16), v_ref[pl.ds(start, tq), sl])
            m_sc[h] = m_new

    def body(j, carry):
        step(j, False)
        return carry

    lax.fori_loop(0, i, body, 0)
    step(i, True)
    outs = []
    for h in range(2):
        a = acc_sc[h]
        outs.append(a[:, :V_DIM] / a[:, V_DIM:V_DIM + 1])
    o_ref[...] = jnp.concatenate(outs, axis=1)


def _flash(q, k, v):
    s = q.shape[0]
    tq = min(FLASH_ROWS, s)
    pairs = N_HEADS // 2
    return pl.pallas_call(
        functools.partial(_flash_kernel, tq=tq),
        out_shape=jax.ShapeDtypeStruct((s, N_HEADS * V_DIM), F32),
        grid=(pairs, s // tq),
        in_specs=[pl.BlockSpec((tq, 2 * LANES), lambda p, i: (i, p)),
                  pl.BlockSpec((s, 2 * LANES), lambda p, i: (0, p)),
                  pl.BlockSpec((s, 2 * LANES), lambda p, i: (0, p))],
        out_specs=pl.BlockSpec((tq, 2 * V_DIM), lambda p, i: (i, p)),
        scratch_shapes=[pltpu.VMEM((2, tq, LANES), F32), pltpu.VMEM((2, tq, LANES), F32)],
        name="flash_prompt", compiler_params=_params(2),
    )(q, k, v)


def _paged_kernel(pt_ref, qlat_ref, qr_ref, cnew_ref, krnew_ref, cache_c, cache_krt, o_ref, cbuf, krbuf, cbf, sem,
                  *, n_pages, page, ppc, t_new):
    b = pl.program_id(0)
    nb = pl.num_programs(0)
    slot = b % 2
    n_chunks = n_pages // ppc
    ck = ppc * page

    def fetch(bb, sl):
        for pg_i in range(n_pages):
            pg = pt_ref[bb, pg_i]
            ci, p = divmod(pg_i, ppc)
            pltpu.make_async_copy(cache_c.at[pg], cbuf.at[sl, pg_i], sem.at[sl]).start()
            pltpu.make_async_copy(cache_krt.at[pg], krbuf.at[sl, ci, :, pl.ds(p * page, page)], sem.at[sl]).start()

    @pl.when(b == 0)
    def _():
        fetch(0, 0)

    @pl.when(b + 1 < nb)
    def _():
        fetch(b + 1, 1 - slot)

    pltpu.make_async_copy(cache_c.at[pl.ds(0, n_pages)], cbuf.at[slot], sem.at[slot]).wait()
    pltpu.make_async_copy(krbuf.at[1 - slot], krbuf.at[slot], sem.at[slot]).wait()

    qlat = qlat_ref[...]
    qr = qr_ref[...]
    rows, kv_lora = qlat.shape

    def scores(ci):
        cb = cbuf[slot, pl.ds(ci * ppc, ppc)].reshape(ck, kv_lora).astype(BF16)
        cbf[ci % 2] = cb
        return _dot_nt(qlat, cb) + _dot(qr, krbuf[slot, ci].astype(BF16))

    def absorb(ci, s, m_prev, l_prev, acc):
        m_new = jnp.maximum(m_prev, jnp.max(s, axis=-1, keepdims=True))
        alpha = jnp.exp2(m_prev - m_new)
        p_ = jnp.exp2(s - m_new)
        l_new = alpha * l_prev + jnp.sum(p_, axis=-1, keepdims=True)
        return m_new, l_new, alpha * acc + _dot(p_.astype(BF16), cbf[ci % 2])

    def chunk(ci, carry):
        s, m_prev, l_prev, acc = carry
        s_next = scores(ci + 1)
        return (s_next,) + absorb(ci, s, m_prev, l_prev, acc)

    init = (scores(0), jnp.full((rows, 1), NEG, F32), jnp.zeros((rows, 1), F32), jnp.zeros((rows, kv_lora), F32))
    s_last, m_old, l_old, acc = lax.fori_loop(0, n_chunks - 1, chunk, init)
    m_old, l_old, acc = absorb(n_chunks - 1, s_last, m_old, l_old, acc)

    ql = qlat.astype(F32)
    qrf = qr.astype(F32)
    cn = cnew_ref[...]
    kn = krnew_ref[...]
    t_row = lax.broadcasted_iota(jnp.int32, (rows, 1), 0) // N_HEADS
    s_new = []
    for j in range(t_new):
        sj = (jnp.sum(ql * cn[j:j + 1, :], axis=-1, keepdims=True)
              + jnp.sum(qrf * kn[j:j + 1, :], axis=-1, keepdims=True))
        s_new.append(jnp.where(t_row >= j, sj, NEG))
    m_fin = m_old
    for sj in s_new:
        m_fin = jnp.maximum(m_fin, sj)
    a = jnp.exp2(m_old - m_fin)
    l_fin = a * l_old
    acc = a * acc
    for j, sj in enumerate(s_new):
        pj = jnp.exp2(sj - m_fin)
        l_fin = l_fin + pj
        acc = acc + pj * cn[j:j + 1, :]
    o_ref[...] = acc / l_fin


def _paged_attention(qlat, qr, c_new, kr_new, cache_c, cache_krt, page_table):
    nb, rows, kv_lora = qlat.shape
    n_pages = page_table.shape[1]
    page = cache_c.shape[1]
    t_new = c_new.shape[1]
    ppc = math.gcd(PAGES_PER_CHUNK, n_pages)
    per_b = lambda r, w: pl.BlockSpec((None, r, w), lambda b, pt: (b, 0, 0))
    any_spec = pl.BlockSpec(memory_space=pl.ANY)
    grid_spec = pltpu.PrefetchScalarGridSpec(
        num_scalar_prefetch=1, grid=(nb,),
        in_specs=[per_b(rows, kv_lora), per_b(rows, QK_ROPE), per_b(t_new, kv_lora), per_b(t_new, QK_ROPE),
                  any_spec, any_spec],
        out_specs=per_b(rows, kv_lora),
        scratch_shapes=[pltpu.VMEM((2, n_pages, page, kv_lora), F32),
                        pltpu.VMEM((2, n_pages // ppc, QK_ROPE, ppc * page), F32),
                        pltpu.VMEM((2, ppc * page, kv_lora), BF16),
                        pltpu.SemaphoreType.DMA((2,))])
    return pl.pallas_call(
        functools.partial(_paged_kernel, n_pages=n_pages, page=page, ppc=ppc, t_new=t_new),
        out_shape=jax.ShapeDtypeStruct((nb, rows, kv_lora), F32), grid_spec=grid_spec, name="paged_sample",
        compiler_params=_params(1),
    )(page_table, qlat, qr, c_new, kr_new, cache_c, cache_krt)


def _back_kernel(*refs, latent_values):
    if latent_values:
        (h_ref, ycn_ref, ya_ref, wuv_ref, gattn_ref, woc_ref, woa_ref, gffn_ref, wrh_ref, wrl_ref, br_ref,
         hp_ref, xn_ref, route_ref, count_ref, base_sc) = refs
        ya = _dot(ya_ref[...].astype(BF16), wuv_ref[...])
    else:
        (h_ref, ycn_ref, ya_ref, gattn_ref, woc_ref, woa_ref, gffn_ref, wrh_ref, wrl_ref, br_ref,
         hp_ref, xn_ref, route_ref, count_ref, base_sc) = refs
        ya = ya_ref[...]
    yan = _rms(ya, gattn_ref[...]).astype(BF16)
    hp = h_ref[...] + _dot(ycn_ref[...], woc_ref[...]) + _dot(yan, woa_ref[...])
    hp_ref[...] = hp
    xn = _rms(hp, gffn_ref[...])
    xn_ref[...] = xn
    x_hi = xn.astype(BF16)
    x_lo = (xn - x_hi.astype(F32)).astype(BF16)
    logits = _dot(x_hi, wrh_ref[...]) + _dot(x_hi, wrl_ref[...]) + _dot(x_lo, wrh_ref[...]) + br_ref[...]
    tm, n_exp = logits.shape
    lane = lax.broadcasted_iota(jnp.int32, logits.shape, 1)
    work = logits
    picks = []
    for _ in range(TOP_K):
        vmax = jnp.max(work, axis=-1, keepdims=True)
        idx = jnp.min(jnp.where(work == vmax, lane, n_exp), axis=-1, keepdims=True)
        sel = lane == idx
        picks.append((vmax, idx, sel))
        work = jnp.where(sel, -jnp.inf, work)
    top = picks[0][0]
    exps = [jnp.exp(v - top) for v, _, _ in picks]
    denom = exps[0]
    for e in exps[1:]:
        denom = denom + e

    @pl.when(pl.program_id(0) == 0)
    def _():
        base_sc[...] = jnp.zeros_like(base_sc)
    onehot = jnp.zeros_like(logits)
    for _, _, sel in picks:
        onehot = jnp.where(sel, 1.0, onehot)
    r_i = lax.broadcasted_iota(jnp.int32, (tm, tm), 0)
    c_i = lax.broadcasted_iota(jnp.int32, (tm, tm), 1)
    earlier = jnp.where(c_i < r_i, 1.0, 0.0).astype(BF16)
    rank_all = _dot(earlier, onehot.astype(BF16)) + base_sc[...]
    base_sc[...] += jnp.sum(onehot, axis=0, keepdims=True)
    count_ref[...] = base_sc[...]

    lane_r = lax.broadcasted_iota(jnp.int32, (tm, LANES), 1)
    route = jnp.zeros((tm, LANES), F32)
    for k, (e, (_, idx, sel)) in enumerate(zip(exps, picks)):
        rank_k = jnp.sum(jnp.where(sel, rank_all, 0.0), axis=-1, keepdims=True)
        route = jnp.where(lane_r == ROUTE_EXPERT + k, idx.astype(F32), route)
        route = jnp.where(lane_r == ROUTE_RANK + k, rank_k, route)
        route = jnp.where(lane_r == ROUTE_GATE + k, e / denom, route)
    route_ref[...] = route


def _back(h, ycn, ya, w, *, latent_values):
    n, d = h.shape
    tm = min(BACK_ROWS, n)
    n_exp = w["wr_hi"].shape[1]
    row = lambda c: pl.BlockSpec((tm, c), lambda i: (i, 0))
    full = lambda a: pl.BlockSpec(a.shape, lambda i: (0,) * a.ndim)
    shared = [w["g_attn"], w["wo_conv"], w["wo_attn"], w["g_ffn"], w["wr_hi"], w["wr_lo"], w["b_router"]]
    if latent_values:
        shared = [w["w_uv_bd"]] + shared
    return pl.pallas_call(
        functools.partial(_back_kernel, latent_values=latent_values),
        out_shape=[jax.ShapeDtypeStruct((n, d), F32), jax.ShapeDtypeStruct((n, d), F32),
                   jax.ShapeDtypeStruct((n, LANES), F32), jax.ShapeDtypeStruct((1, n_exp), F32)],
        grid=(n // tm,),
        in_specs=[row(d), row(ycn.shape[1]), row(ya.shape[1])] + [full(a) for a in shared],
        out_specs=[row(d), row(d), row(LANES), pl.BlockSpec((1, n_exp), lambda i: (0, 0))],
        scratch_shapes=[pltpu.VMEM((1, n_exp), F32)],
        name="back_sample" if latent_values else "back_prompt", compiler_params=_params(1),
    )(h, ycn, ya, *shared)


def _dispatch_kernel(slot_ref, x_ref, xs_in_ref, xs_ref, sem):
    del xs_in_ref
    tm = x_ref.shape[0]

    def body(r, carry):
        for k in range(TOP_K):
            s = slot_ref[r * TOP_K + k]
            pltpu.make_async_copy(x_ref.at[pl.ds(r, 1)], xs_ref.at[pl.ds(s, 1)], sem).start(priority=k % 2)
        return carry

    lax.fori_loop(0, tm, body, 0, unroll=4)
    for k in range(TOP_K):
        pltpu.make_async_copy(x_ref, xs_ref.at[pl.ds(0, tm)], sem).wait()


def _dispatch(x, slots, xs):
    n, d = x.shape
    tm = min(DISPATCH_ROWS, n)
    return pl.pallas_call(
        _dispatch_kernel,
        out_shape=jax.ShapeDtypeStruct(xs.shape, F32),
        grid=(n // tm,),
        in_specs=[pl.BlockSpec((tm * TOP_K,), lambda i: (i,), memory_space=pltpu.SMEM),
                  pl.BlockSpec((tm, d), lambda i: (i, 0)),
                  pl.BlockSpec(memory_space=pl.ANY)],
        out_specs=pl.BlockSpec(memory_space=pl.ANY),
        scratch_shapes=[pltpu.SemaphoreType.DMA(())],
        input_output_aliases={2: 0},
        name="moe_dispatch", compiler_params=_params(1),
    )(slots, x, xs)


def _expert_kernel(te_ref, nu_ref, xs_ref, wgu_ref, bgu_ref, wd_ref, bd_ref, ys_ref, wgu_bf, wd_bf):
    ti = pl.program_id(0)
    used = ti < nu_ref[0]
    new_expert = (ti == 0) | (te_ref[ti] != te_ref[jnp.maximum(ti - 1, 0)])
    d_ff = wd_ref.shape[0]

    @pl.when(used & new_expert)
    def _():
        wgu_bf[...] = wgu_ref[...].astype(BF16)
        wd_bf[...] = wd_ref[...].astype(BF16)

    @pl.when(used)
    def _():
        x = xs_ref[...].astype(BF16)
        bgu = bgu_ref[...]
        gate = jnp.minimum(_dot(x, wgu_bf[:, :d_ff]) + bgu[:, :d_ff], SWIGLU_LIMIT)
        up = jnp.clip(_dot(x, wgu_bf[:, d_ff:]) + bgu[:, d_ff:], -SWIGLU_LIMIT, SWIGLU_LIMIT)
        hid = (up + 1.0) * gate * jax.nn.sigmoid(SWIGLU_ALPHA * gate)
        ys_ref[...] = _dot(hid.astype(BF16), wd_bf[...]) + bd_ref[...]

    @pl.when(jnp.logical_not(used))
    def _():
        ys_ref[...] = jnp.zeros_like(ys_ref)


def _experts(xs, tile_expert, n_used, w_gu, b_gu, w_d, b_d):
    n_slots, d = xs.shape
    n_exp, _, d_ff2 = w_gu.shape
    d_ff = d_ff2 // 2
    tmm = MOE_TILE
    rows = lambda t, te, nu: (jnp.minimum(t, nu[0] - 1), 0)
    grid_spec = pltpu.PrefetchScalarGridSpec(
        num_scalar_prefetch=2, grid=(n_slots // tmm,),
        in_specs=[pl.BlockSpec((tmm, d), rows),
                  pl.BlockSpec((None, d, d_ff2), lambda t, te, nu: (te[t], 0, 0)),
                  pl.BlockSpec((None, 1, d_ff2), lambda t, te, nu: (te[t], 0, 0)),
                  pl.BlockSpec((None, d_ff, d), lambda t, te, nu: (te[t], 0, 0)),
                  pl.BlockSpec((None, 1, d), lambda t, te, nu: (te[t], 0, 0))],
        out_specs=pl.BlockSpec((tmm, d), lambda t, te, nu: (t, 0)),
        scratch_shapes=[pltpu.VMEM((d, d_ff2), BF16), pltpu.VMEM((d_ff, d), BF16)])
    return pl.pallas_call(
        _expert_kernel, out_shape=jax.ShapeDtypeStruct((n_slots, d), F32), grid_spec=grid_spec,
        name="moe_experts", compiler_params=_params(1),
    )(tile_expert, n_used, xs, w_gu, b_gu.reshape(n_exp, 1, d_ff2), w_d, b_d.reshape(n_exp, 1, d))


def _combine_kernel(slot_ref, route_ref, hp_ref, ys_ref, gfin_ref, o_ref, buf, sem):
    tm = hp_ref.shape[0]

    def body(r, carry):
        for k in range(TOP_K):
            s = slot_ref[r * TOP_K + k]
            pltpu.make_async_copy(ys_ref.at[pl.ds(s, 1)], buf.at[k, pl.ds(r, 1)], sem).start(priority=k % 2)
        return carry

    lax.fori_loop(0, tm, body, 0, unroll=4)
    for k in range(TOP_K):
        pltpu.make_async_copy(ys_ref.at[pl.ds(0, tm)], buf.at[k], sem).wait()
    route = route_ref[...]
    acc = hp_ref[...]
    for k in range(TOP_K):
        acc = acc + route[:, ROUTE_GATE + k:ROUTE_GATE + k + 1] * buf[k]
    o_ref[...] = _rms(acc, gfin_ref[...])


def _combine(slots, route, hp, ys, g_fin):
    n, d = hp.shape
    tm = min(COMBINE_ROWS, n)
    return pl.pallas_call(
        _combine_kernel,
        out_shape=jax.ShapeDtypeStruct((n, d), F32),
        grid=(n // tm,),
        in_specs=[pl.BlockSpec((tm * TOP_K,), lambda i: (i,), memory_space=pltpu.SMEM),
                  pl.BlockSpec((tm, LANES), lambda i: (i, 0)),
                  pl.BlockSpec((tm, d), lambda i: (i, 0)),
                  pl.BlockSpec(memory_space=pl.ANY),
                  pl.BlockSpec((1, d), lambda i: (0, 0))],
        out_specs=pl.BlockSpec((tm, d), lambda i: (i, 0)),
        scratch_shapes=[pltpu.VMEM((TOP_K, tm, d), F32), pltpu.SemaphoreType.DMA(())],
        name="moe_combine", compiler_params=_params(1),
    )(slots, route, hp, ys, g_fin)


def _slot_tables(routes, counts, n_exp):
    tmm = MOE_TILE
    counts = [c.reshape(n_exp).astype(jnp.int32) for c in counts]
    total = sum(counts)
    group = ((total + tmm - 1) // tmm) * tmm
    ends = jnp.cumsum(group)
    start = ends - group
    slots = []
    before = jnp.zeros_like(total)
    experts = jnp.arange(n_exp, dtype=jnp.int32)
    for route, cnt in zip(routes, counts):
        eid = route[:, ROUTE_EXPERT:ROUTE_EXPERT + TOP_K].astype(jnp.int32)
        rank = route[:, ROUTE_RANK:ROUTE_RANK + TOP_K].astype(jnp.int32)
        first = jnp.sum(jnp.where(eid[:, :, None] == experts, start + before, 0), axis=-1)
        slots.append((first + rank).reshape(-1))
        before = before + cnt
    n_tokens = sum(r.shape[0] for r in routes)
    max_tiles = (n_tokens * TOP_K) // tmm + n_exp
    n_used = ends[-1] // tmm
    tile_start = jnp.minimum(jnp.arange(max_tiles, dtype=jnp.int32) * tmm, ends[-1] - 1)
    tile_expert = jnp.sum((ends[None, :] <= tile_start[:, None]).astype(jnp.int32), axis=1)
    return slots, tile_expert, n_used.reshape(1).astype(jnp.int32), max_tiles * tmm


def _prep_layer(norm_mix_g, w_in, conv_w, q_norm_g, kv_norm_g, w_uq, w_ukv, out_norm_conv_g, out_norm_attn_g,
                w_out, norm_ffn_g, w_router, b_router):
    d = w_in.shape[0]
    d_conv = conv_w.shape[1]
    q_lora = q_norm_g.shape[0]
    kv_lora = kv_norm_g.shape[0]
    c_kr = 3 * d_conv + q_lora + kv_lora
    w_kr = w_in[:, c_kr:c_kr + QK_ROPE]
    w_kr_rot = jnp.concatenate([-w_kr[:, HALF_ROPE:], w_kr[:, :HALF_ROPE]], axis=1)
    w_in_pad = jnp.concatenate([w_in[:, :c_kr], w_kr, w_kr_rot, jnp.zeros((d, LANES - 2 * QK_ROPE), F32)], axis=1)

    qk = QK_NOPE + QK_ROPE
    wq = w_uq.reshape(q_lora, N_HEADS, qk)
    pad = jnp.zeros((q_lora, N_HEADS, LANES - qk), F32)
    wq_pad = jnp.concatenate([wq, pad], axis=2)
    r1 = wq[:, :, QK_NOPE:QK_NOPE + HALF_ROPE]
    r2 = wq[:, :, QK_NOPE + HALF_ROPE:]
    wq_rot = jnp.concatenate([jnp.zeros((q_lora, N_HEADS, QK_NOPE), F32), -r2, r1, pad], axis=2)
    wq2 = jnp.concatenate([wq_pad.reshape(q_lora, -1), wq_rot.reshape(q_lora, -1)], axis=1)

    wkv = w_ukv.reshape(kv_lora, N_HEADS, QK_NOPE + V_DIM)
    w_uk = wkv[:, :, :QK_NOPE]
    w_uv = wkv[:, :, QK_NOPE:]
    wk_pad = jnp.concatenate([w_uk, jnp.zeros((kv_lora, N_HEADS, LANES - QK_NOPE), F32)], axis=2)
    wv_pad = jnp.concatenate([w_uv, jnp.zeros((kv_lora, N_HEADS, LANES - V_DIM), F32)], axis=2)
    v_one = jnp.zeros((1, N_HEADS, LANES), F32).at[:, :, V_DIM].set(1.0)
    place = jnp.zeros((QK_ROPE, N_HEADS, LANES), F32)
    place = place.at[jnp.arange(QK_ROPE), :, QK_NOPE + jnp.arange(QK_ROPE)].set(1.0)
    wuk_t = jnp.concatenate([jnp.transpose(w_uk, (1, 2, 0)),
                             jnp.zeros((N_HEADS, LANES - QK_NOPE, kv_lora), F32)], axis=1)
    eye = jnp.eye(N_HEADS, dtype=F32)
    w_uv_bd = jnp.einsum("chv,hg->hcgv", w_uv, eye).reshape(N_HEADS * kv_lora, N_HEADS * V_DIM)
    wr_hi = w_router.astype(BF16)
    wr_lo = (w_router - wr_hi.astype(F32)).astype(BF16)
    return {
        "d_conv": d_conv, "q_lora": q_lora, "kv_lora": kv_lora,
        "g_mix": norm_mix_g.reshape(1, -1), "w_in": w_in_pad.astype(BF16), "conv_w": conv_w,
        "g_q": q_norm_g.reshape(1, -1), "g_kv": kv_norm_g.reshape(1, -1), "wq2": wq2.astype(BF16),
        "g_conv": out_norm_conv_g.reshape(1, -1), "wk_pad": wk_pad.reshape(kv_lora, -1).astype(BF16),
        "place": place.reshape(QK_ROPE, -1).astype(BF16), "wv_pad": wv_pad.reshape(kv_lora, -1).astype(BF16),
        "v_one": v_one.reshape(1, -1), "wuk_t": wuk_t.astype(BF16),
        "g_attn": out_norm_attn_g.reshape(1, -1), "wo_conv": w_out[:d_conv, :].astype(BF16),
        "wo_attn": w_out[d_conv:, :].astype(BF16), "w_uv_bd": w_uv_bd.astype(BF16),
        "g_ffn": norm_ffn_g.reshape(1, -1), "wr_hi": wr_hi, "wr_lo": wr_lo, "b_router": b_router.reshape(1, -1),
    }


def kernel(x_prompt, x_sample, cache_kv_latent, cache_k_rope, state_conv, page_table, norm_mix_g, w_in, conv_w,
           q_norm_g, kv_norm_g, w_uq, w_ukv, out_norm_conv_g, out_norm_attn_g, w_out, norm_ffn_g, w_router,
           b_router, w_gate_up, b_gate_up, w_down, b_down, final_norm_g):
    depth = w_in.shape[0]
    batch, seq, d = x_prompt.shape
    dec_b, dec_t, _ = x_sample.shape
    n_exp = w_router.shape[-1]
    past_len = page_table.shape[1] * cache_kv_latent.shape[2]
    assert batch == 1, "the prompt path lays one sequence out along rows"
    assert depth == 1, "deeper stacks need the un-normalised residual between layers"
    pos_p = jnp.arange(seq, dtype=jnp.int32)
    pos_s = jnp.tile(past_len + jnp.arange(dec_t, dtype=jnp.int32), dec_b)
    hp = x_prompt.reshape(seq, d)
    hs = x_sample.reshape(dec_b * dec_t, d)
    g_fin = final_norm_g.reshape(1, -1)
    outs = [[] for _ in range(6)]
    for l in range(depth):
        w = _prep_layer(norm_mix_g[l], w_in[l], conv_w[l], q_norm_g[l], kv_norm_g[l], w_uq[l], w_ukv[l],
                        out_norm_conv_g[l], out_norm_attn_g[l], w_out[l], norm_ffn_g[l], w_router[l], b_router[l])
        d_conv, kv_lora = w["d_conv"], w["kv_lora"]

        u_p, ycn_p, q_p, ckv_p, kr_p, k_p, v_p = _front(hp, pos_p, w, prompt=True)
        ya_p = _flash(q_p, k_p, v_p)
        hp_mid, xn_p, route_p, count_p = _back(hp, ycn_p, ya_p, w, latent_values=False)

        st = state_conv[l]
        t_idx = jnp.arange(dec_t)
        ov1 = jnp.where((t_idx == 0)[None, :, None], st[:, 1:2, :], 0.0).reshape(dec_b * dec_t, d_conv)
        ov2 = jnp.where((t_idx == 0)[None, :, None], st[:, 0:1, :],
                        jnp.where((t_idx == 1)[None, :, None], st[:, 1:2, :], 0.0)).reshape(dec_b * dec_t, d_conv)
        u_s, ycn_s, q_s, ckv_s, kr_s, qlat_s = _front(hs, pos_s, w, prompt=False, ov=(ov1, ov2), seq_rows=dec_t)
        rows = dec_t * N_HEADS
        qr_s = q_s.reshape(dec_b, rows, LANES)[:, :, QK_NOPE:QK_NOPE + QK_ROPE]
        cache_krt = jnp.swapaxes(cache_k_rope[l], 1, 2)
        o_lat = _paged_attention(qlat_s.reshape(dec_b, rows, kv_lora), qr_s,
                                 ckv_s.reshape(dec_b, dec_t, kv_lora), kr_s.reshape(dec_b, dec_t, QK_ROPE),
                                 cache_kv_latent[l], cache_krt, page_table)
        ya_s = o_lat.reshape(dec_b * dec_t, N_HEADS * kv_lora)
        hs_mid, xn_s, route_s, count_s = _back(hs, ycn_s, ya_s, w, latent_values=True)

        (slot_p, slot_s), tile_expert, n_used, n_slots = _slot_tables((route_p, route_s), (count_p, count_s), n_exp)
        xs = jnp.zeros((n_slots, d), F32)
        xs = _dispatch(xn_p, slot_p, xs)
        xs = _dispatch(xn_s, slot_s, xs)
        ys = _experts(xs, tile_expert, n_used, w_gate_up[l], b_gate_up[l], w_down[l], b_down[l])
        hp = _combine(slot_p, route_p, hp_mid, ys, g_fin)
        hs = _combine(slot_s, route_s, hs_mid, ys, g_fin)

        outs[0].append(ckv_p.reshape(batch, seq, kv_lora))
        outs[1].append(kr_p.reshape(batch, seq, QK_ROPE))
        outs[2].append(u_p[seq - 2:, :].reshape(batch, 2, d_conv))
        outs[3].append(ckv_s.reshape(dec_b, dec_t, kv_lora))
        outs[4].append(kr_s.reshape(dec_b, dec_t, QK_ROPE))
        outs[5].append(u_s.reshape(dec_b, dec_t, d_conv)[:, dec_t - 2:, :])
    return (hp.reshape(batch, seq, d), hs.reshape(dec_b, dec_t, d)) + tuple(jnp.stack(o) for o in outs)
```

```python
import functools
import math

import jax
import jax.numpy as jnp
from jax import lax
from jax.experimental import pallas as pl
from jax.experimental.pallas import tpu as pltpu

N_HEADS = 8
QK_NOPE = 64
QK_ROPE = 32
V_DIM = 64
HALF_ROPE = QK_ROPE // 2
ROPE_THETA = 10000.0
SOFTMAX_SCALE = (QK_NOPE + QK_ROPE) ** -0.5
LOG2E = math.log2(math.e)
TOP_K = 4
SWIGLU_LIMIT = 7.0
SWIGLU_ALPHA = 1.702
RMS_EPS = 1e-6
NEG = -1e30

LANES = 128
MIB = 1024 * 1024
VMEM_LIMIT = 56 * MIB
F32 = jnp.float32
BF16 = jnp.bfloat16

FRONT_ROWS = 512
FLASH_ROWS = 1024
BACK_ROWS = 512
PAGES_PER_CHUNK = 16
MOE_TILE = 512
DISPATCH_ROWS = 512
COMBINE_ROWS = 512
ROUTE_EXPERT, ROUTE_RANK, ROUTE_GATE = 0, TOP_K, 2 * TOP_K

NT_DIMS = (((1,), (1,)), ((), ()))


def _rms(x, g):
    return x * lax.rsqrt(jnp.mean(x * x, axis=-1, keepdims=True) + RMS_EPS) * g


def _dot(a, b):
    return jnp.dot(a, b, preferred_element_type=F32)


def _dot_nt(a, b):
    return lax.dot_general(a, b, NT_DIMS, preferred_element_type=F32)


def _params(n_axes):
    return pltpu.CompilerParams(dimension_semantics=("arbitrary",) * n_axes, vmem_limit_bytes=VMEM_LIMIT)


def _front_kernel(*refs, d_conv, q_lora, kv_lora, seq_rows, prompt):
    if prompt:
        (x_ref, cs_ref, gmix_ref, win_ref, cw_ref, gq_ref, gkv_ref, wq_ref, gconv_ref,
         wk_ref, place_ref, wv_ref, vone_ref,
         u_ref, ycn_ref, q_ref, ckv_ref, kr_ref, k_ref, v_ref, carry_ref) = refs
    else:
        (x_ref, cs_ref, ov1_ref, ov2_ref, gmix_ref, win_ref, cw_ref, gq_ref, gkv_ref, wq_ref,
         gconv_ref, wukt_ref,
         u_ref, ycn_ref, q_ref, ckv_ref, kr_ref, qlat_ref) = refs
    tm = x_ref.shape[0]
    n = _rms(x_ref[...], gmix_ref[...]).astype(BF16)
    z = _dot(n, win_ref[...])
    c0 = 3 * d_conv
    b_g = z[:, 0:d_conv]
    u = z[:, d_conv:2 * d_conv] * z[:, 2 * d_conv:c0]
    z_q = z[:, c0:c0 + q_lora]
    z_kv = z[:, c0 + q_lora:c0 + q_lora + kv_lora]
    z_r = z[:, c0 + q_lora + kv_lora:]
    u_ref[...] = u

    row = lax.broadcasted_iota(jnp.int32, u.shape, 0)
    r1 = pltpu.roll(u, 1, 0)
    r2 = pltpu.roll(u, 2, 0)
    if prompt:
        @pl.when(pl.program_id(0) == 0)
        def _():
            carry_ref[...] = jnp.zeros_like(carry_ref)
        last1 = carry_ref[7:8, :]
        last2 = carry_ref[6:7, :]
        p1 = jnp.where(row == 0, last1, r1)
        p2 = jnp.where(row == 0, last2, jnp.where(row == 1, last1, r2))
        carry_ref[...] = u[tm - 8:tm, :]
    else:
        t = row % seq_rows
        p1 = jnp.where(t == 0, ov1_ref[...], r1)
        p2 = jnp.where(t < 2, ov2_ref[...], r2)
    cw = cw_ref[...]
    y_c = p2 * cw[0:1, :] + p1 * cw[1:2, :] + u * cw[2:3, :]
    ycn_ref[...] = _rms(b_g * y_c, gconv_ref[...]).astype(BF16)

    nq = _rms(z_q, gq_ref[...]).astype(BF16)
    q2 = _dot(nq, wq_ref[...])
    hw = N_HEADS * LANES
    cs = cs_ref[...]
    cos, sin = cs[:, :HALF_ROPE], cs[:, HALF_ROPE:]
    qs = SOFTMAX_SCALE * LOG2E
    pad = jnp.zeros((tm, LANES - QK_NOPE - QK_ROPE), F32)
    tqc = jnp.concatenate([jnp.full((tm, QK_NOPE), qs, F32), cos * qs, cos * qs, pad], axis=1)
    tqs = jnp.concatenate([jnp.zeros((tm, QK_NOPE), F32), sin * qs, sin * qs, pad], axis=1)
    q_heads = []
    for h in range(N_HEADS):
        qa = q2[:, h * LANES:(h + 1) * LANES]
        qb = q2[:, hw + h * LANES:hw + (h + 1) * LANES]
        q_heads.append((qa * tqc + qb * tqs).astype(BF16))
    q_ref[...] = jnp.concatenate(q_heads, axis=1)

    ckv = _rms(z_kv, gkv_ref[...])
    ckv_ref[...] = ckv
    tk = jnp.concatenate([cos, cos, sin, sin, jnp.zeros((tm, LANES - 2 * QK_ROPE), F32)], axis=1)
    zr = z_r * tk
    kr = zr[:, 0:QK_ROPE] + zr[:, QK_ROPE:2 * QK_ROPE]
    kr_ref[...] = kr

    if prompt:
        ckv_b = ckv.astype(BF16)
        k_ref[...] = (_dot(ckv_b, wk_ref[...]) + _dot(kr.astype(BF16), place_ref[...])).astype(BF16)
        v_ref[...] = (_dot(ckv_b, wv_ref[...]) + vone_ref[...]).astype(BF16)
    else:
        for h in range(N_HEADS):
            qlat_ref[:, h * kv_lora:(h + 1) * kv_lora] = _dot(q_heads[h], wukt_ref[h]).astype(BF16)


def _front(x, pos, w, *, prompt, ov=None, seq_rows=0):
    n, d = x.shape
    d_conv, q_lora, kv_lora = w["d_conv"], w["q_lora"], w["kv_lora"]
    tm = min(FRONT_ROWS, n)
    hw = N_HEADS * LANES
    inv_freq = ROPE_THETA ** (-jnp.arange(HALF_ROPE, dtype=F32) / HALF_ROPE)
    ang = pos.astype(F32)[:, None] * inv_freq[None, :]
    cs = jnp.concatenate([jnp.cos(ang), jnp.sin(ang)], axis=1)

    row = lambda c: pl.BlockSpec((tm, c), lambda i: (i, 0))
    full = lambda a: pl.BlockSpec(a.shape, lambda i: (0,) * a.ndim)
    ins = [x, cs]
    in_specs = [row(d), row(2 * HALF_ROPE)]
    if not prompt:
        ins += [ov[0], ov[1]]
        in_specs += [row(d_conv), row(d_conv)]
    shared = [w["g_mix"], w["w_in"], w["conv_w"], w["g_q"], w["g_kv"], w["wq2"], w["g_conv"]]
    shared += [w["wk_pad"], w["place"], w["wv_pad"], w["v_one"]] if prompt else [w["wuk_t"]]
    ins += shared
    in_specs += [full(a) for a in shared]

    outs = [jax.ShapeDtypeStruct((n, d_conv), F32), jax.ShapeDtypeStruct((n, d_conv), BF16),
            jax.ShapeDtypeStruct((n, hw), BF16), jax.ShapeDtypeStruct((n, kv_lora), F32),
            jax.ShapeDtypeStruct((n, QK_ROPE), F32)]
    out_specs = [row(d_conv), row(d_conv), row(hw), row(kv_lora), row(QK_ROPE)]
    scratch = []
    if prompt:
        outs += [jax.ShapeDtypeStruct((n, hw), BF16), jax.ShapeDtypeStruct((n, hw), BF16)]
        out_specs += [row(hw), row(hw)]
        scratch = [pltpu.VMEM((8, d_conv), F32)]
    else:
        outs += [jax.ShapeDtypeStruct((n, N_HEADS * kv_lora), BF16)]
        out_specs += [row(N_HEADS * kv_lora)]
    kern = functools.partial(_front_kernel, d_conv=d_conv, q_lora=q_lora, kv_lora=kv_lora, seq_rows=seq_rows,
                             prompt=prompt)
    return pl.pallas_call(
        kern, out_shape=outs, grid=(n // tm,), in_specs=in_specs, out_specs=out_specs, scratch_shapes=scratch,
        name="front_prompt" if prompt else "front_sample", compiler_params=_params(1),
    )(*ins)


def _flash_kernel(q_ref, k_ref, v_ref, o_ref, m_sc, acc_sc, *, tq):
    i = pl.program_id(1)
    m_sc[...] = jnp.full_like(m_sc, NEG)
    acc_sc[...] = jnp.zeros_like(acc_sc)
    half = tq // 2

    def block(row0, n_rows, key0, n_keys, diagonal):
        rows = slice(row0, row0 + n_rows)
        for h in range(2):
            sl = slice(h * LANES, (h + 1) * LANES)
            s = _dot_nt(q_ref[rows, sl], k_ref[pl.ds(key0, n_keys), sl])
            if diagonal:
                r = lax.broadcasted_iota(jnp.int32, s.shape, 0)
                c = lax.broadcasted_iota(jnp.int32, s.shape, 1)
                s = jnp.where(c <= r, s, NEG)
            m_prev = m_sc[h, rows]
            m_new = jnp.maximum(m_prev, jnp.max(s, axis=-1, keepdims=True))
            alpha = jnp.exp2(m_prev - m_new)
            p = jnp.exp2(s - jnp.concatenate([m_new] * (n_keys // LANES), axis=1))
            acc_sc[h, rows] = alpha * acc_sc[h, rows] + _dot(p.astype(BF16), v_ref[pl.ds(key0, n_keys), sl])
            m_sc[h, rows] = m_new

    def body(j, carry):
        block(0, tq, pl.multiple_of(j * tq, tq), tq, False)
        return carry

    lax.fori_loop(0, i, body, 0)
    diag = pl.multiple_of(i * tq, tq)
    block(0, tq, diag, half, True)
    block(half, half, diag + half, half, True)
    outs = []
    for h in range(2):
        a = acc_sc[h]
        outs.append(a[:, :V_DIM] / a[:, V_DIM:V_DIM + 1])
    o_ref[...] = jnp.concatenate(outs, axis=1)


def _flash(q, k, v):
    s = q.shape[0]
    tq = min(FLASH_ROWS, s)
    pairs = N_HEADS // 2
    return pl.pallas_call(
        functools.partial(_flash_kernel, tq=tq),
        out_shape=jax.ShapeDtypeStruct((s, N_HEADS * V_DIM), F32),
        grid=(pairs, s // tq),
        in_specs=[pl.BlockSpec((tq, 2 * LANES), lambda p, i: (i, p)),
                  pl.BlockSpec((s, 2 * LANES), lambda p, i: (0, p)),
                  pl.BlockSpec((s, 2 * LANES), lambda p, i: (0, p))],
        out_specs=pl.BlockSpec((tq, 2 * V_DIM), lambda p, i: (i, p)),
        scratch_shapes=[pltpu.VMEM((2, tq, LANES), F32), pltpu.VMEM((2, tq, LANES), F32)],
        name="flash_prompt", compiler_params=_params(2),
    )(q, k, v)


def _paged_kernel(pt_ref, qlat_ref, qr_ref, cnew_ref, krnew_ref, cache_c, cache_krt, o_ref, cbuf, krbuf, cbf, sem,
                  *, nb, n_pages, page, ppc, t_new):
    b = pl.program_id(0)
    slot = b % 2
    n_chunks = n_pages // ppc
    ck = ppc * page

    def fetch(bb, sl):
        for pg_i in range(n_pages):
            pg = pt_ref[bb, pg_i]
            ci, p = divmod(pg_i, ppc)
            pltpu.make_async_copy(cache_c.at[pg], cbuf.at[sl, pg_i], sem.at[sl]).start()
            pltpu.make_async_copy(cache_krt.at[pg], krbuf.at[sl, ci, :, pl.ds(p * page, page)], sem.at[sl]).start()

    @pl.when(b == 0)
    def _():
        fetch(0, 0)

    if nb > 1:
        @pl.when(b + 1 < nb)
        def _():
            fetch(b + 1, 1 - slot)

    pltpu.make_async_copy(cache_c.at[pl.ds(0, n_pages)], cbuf.at[slot], sem.at[slot]).wait()
    pltpu.make_async_copy(krbuf.at[1 - slot], krbuf.at[slot], sem.at[slot]).wait()

    qlat = qlat_ref[...]
    qr = qr_ref[...]
    rows, kv_lora = qlat.shape

    def scores(ci):
        cb = cbuf[slot, pl.ds(ci * ppc, ppc)].reshape(ck, kv_lora).astype(BF16)
        cbf[ci % 2] = cb
        return _dot_nt(qlat, cb) + _dot(qr, krbuf[slot, ci].astype(BF16))

    def absorb(ci, s, m_prev, l_prev, acc):
        m_new = jnp.maximum(m_prev, jnp.max(s, axis=-1, keepdims=True))
        alpha = jnp.exp2(m_prev - m_new)
        p_ = jnp.exp2(s - m_new)
        l_new = alpha * l_prev + jnp.sum(p_, axis=-1, keepdims=True)
        return m_new, l_new, alpha * acc + _dot(p_.astype(BF16), cbf[ci % 2])

    def chunk(ci, carry):
        s, m_prev, l_prev, acc = carry
        s_next = scores(ci + 1)
        return (s_next,) + absorb(ci, s, m_prev, l_prev, acc)

    init = (scores(0), jnp.full((rows, 1), NEG, F32), jnp.zeros((rows, 1), F32), jnp.zeros((rows, kv_lora), F32))
    s_last, m_old, l_old, acc = lax.fori_loop(0, n_chunks - 1, chunk, init)
    m_old, l_old, acc = absorb(n_chunks - 1, s_last, m_old, l_old, acc)

    ql = qlat.astype(F32)
    qrf = qr.astype(F32)
    cn = cnew_ref[...]
    kn = krnew_ref[...]
    t_row = lax.broadcasted_iota(jnp.int32, (rows, 1), 0) // N_HEADS
    s_new = []
    for j in range(t_new):
        sj = (jnp.sum(ql * cn[j:j + 1, :], axis=-1, keepdims=True)
              + jnp.sum(qrf * kn[j:j + 1, :], axis=-1, keepdims=True))
        s_new.append(jnp.where(t_row >= j, sj, NEG))
    m_fin = m_old
    for sj in s_new:
        m_fin = jnp.maximum(m_fin, sj)
    a = jnp.exp2(m_old - m_fin)
    l_fin = a * l_old
    acc = a * acc
    for j, sj in enumerate(s_new):
        pj = jnp.exp2(sj - m_fin)
        l_fin = l_fin + pj
        acc = acc + pj * cn[j:j + 1, :]
    o_ref[...] = acc / l_fin


def _paged_attention(qlat, qr, c_new, kr_new, cache_c, cache_krt, page_table):
    nb, rows, kv_lora = qlat.shape
    n_pages = page_table.shape[1]
    page = cache_c.shape[1]
    t_new = c_new.shape[1]
    ppc = math.gcd(PAGES_PER_CHUNK, n_pages)
    per_b = lambda r, w: pl.BlockSpec((None, r, w), lambda b, pt: (b, 0, 0))
    any_spec = pl.BlockSpec(memory_space=pl.ANY)
    grid_spec = pltpu.PrefetchScalarGridSpec(
        num_scalar_prefetch=1, grid=(nb,),
        in_specs=[per_b(rows, kv_lora), per_b(rows, QK_ROPE), per_b(t_new, kv_lora), per_b(t_new, QK_ROPE),
                  any_spec, any_spec],
        out_specs=per_b(rows, kv_lora),
        scratch_shapes=[pltpu.VMEM((2, n_pages, page, kv_lora), F32),
                        pltpu.VMEM((2, n_pages // ppc, QK_ROPE, ppc * page), F32),
                        pltpu.VMEM((2, ppc * page, kv_lora), BF16),
                        pltpu.SemaphoreType.DMA((2,))])
    return pl.pallas_call(
        functools.partial(_paged_kernel, nb=nb, n_pages=n_pages, page=page, ppc=ppc, t_new=t_new),
        out_shape=jax.ShapeDtypeStruct((nb, rows, kv_lora), F32), grid_spec=grid_spec, name="paged_sample",
        compiler_params=_params(1),
    )(page_table, qlat, qr, c_new, kr_new, cache_c, cache_krt)


def _back_kernel(*refs, latent_values):
    if latent_values:
        (h_ref, ycn_ref, ya_ref, wuv_ref, gattn_ref, woc_ref, woa_ref, gffn_ref, wrh_ref, wrl_ref, br_ref,
         hp_ref, xn_ref, route_ref, count_ref, base_sc) = refs
        ya = _dot(ya_ref[...].astype(BF16), wuv_ref[...])
    else:
        (h_ref, ycn_ref, ya_ref, gattn_ref, woc_ref, woa_ref, gffn_ref, wrh_ref, wrl_ref, br_ref,
         hp_ref, xn_ref, route_ref, count_ref, base_sc) = refs
        ya = ya_ref[...]
    yan = _rms(ya, gattn_ref[...]).astype(BF16)
    hp = h_ref[...] + _dot(ycn_ref[...], woc_ref[...]) + _dot(yan, woa_ref[...])
    hp_ref[...] = hp
    xn = _rms(hp, gffn_ref[...])
    xn_ref[...] = xn
    x_hi = xn.astype(BF16)
    x_lo = (xn - x_hi.astype(F32)).astype(BF16)
    logits = _dot(x_hi, wrh_ref[...]) + _dot(x_hi, wrl_ref[...]) + _dot(x_lo, wrh_ref[...]) + br_ref[...]
    tm, n_exp = logits.shape
    lane = lax.broadcasted_iota(jnp.int32, logits.shape, 1)
    work = logits
    picks = []
    for _ in range(TOP_K):
        vmax = jnp.max(work, axis=-1, keepdims=True)
        idx = jnp.min(jnp.where(work == vmax, lane, n_exp), axis=-1, keepdims=True)
        sel = lane == idx
        picks.append((vmax, idx, sel))
        work = jnp.where(sel, -jnp.inf, work)
    top = picks[0][0]
    exps = [jnp.exp(v - top) for v, _, _ in picks]
    denom = exps[0]
    for e in exps[1:]:
        denom = denom + e

    @pl.when(pl.program_id(0) == 0)
    def _():
        base_sc[...] = jnp.zeros_like(base_sc)
    onehot = jnp.zeros_like(logits)
    for _, _, sel in picks:
        onehot = jnp.where(sel, 1.0, onehot)
    r_i = lax.broadcasted_iota(jnp.int32, (tm, tm), 0)
    c_i = lax.broadcasted_iota(jnp.int32, (tm, tm), 1)
    earlier = jnp.where(c_i < r_i, 1.0, 0.0).astype(BF16)
    rank_all = _dot(earlier, onehot.astype(BF16)) + base_sc[...]
    base_sc[...] += jnp.sum(onehot, axis=0, keepdims=True)
    count_ref[...] = base_sc[...]

    lane_r = lax.broadcasted_iota(jnp.int32, (tm, LANES), 1)
    route = jnp.zeros((tm, LANES), F32)
    for k, (e, (_, idx, sel)) in enumerate(zip(exps, picks)):
        rank_k = jnp.sum(jnp.where(sel, rank_all, 0.0), axis=-1, keepdims=True)
        route = jnp.where(lane_r == ROUTE_EXPERT + k, idx.astype(F32), route)
        route = jnp.where(lane_r == ROUTE_RANK + k, rank_k, route)
        route = jnp.where(lane_r == ROUTE_GATE + k, e / denom, route)
    route_ref[...] = route


def _back(h, ycn, ya, w, *, latent_values):
    n, d = h.shape
    tm = min(BACK_ROWS, n)
    n_exp = w["wr_hi"].shape[1]
    row = lambda c: pl.BlockSpec((tm, c), lambda i: (i, 0))
    full = lambda a: pl.BlockSpec(a.shape, lambda i: (0,) * a.ndim)
    shared = [w["g_attn"], w["wo_conv"], w["wo_attn"], w["g_ffn"], w["wr_hi"], w["wr_lo"], w["b_router"]]
    if latent_values:
        shared = [w["w_uv_bd"]] + shared
    return pl.pallas_call(
        functools.partial(_back_kernel, latent_values=latent_values),
        out_shape=[jax.ShapeDtypeStruct((n, d), F32), jax.ShapeDtypeStruct((n, d), F32),
                   jax.ShapeDtypeStruct((n, LANES), F32), jax.ShapeDtypeStruct((1, n_exp), F32)],
        grid=(n // tm,),
        in_specs=[row(d), row(ycn.shape[1]), row(ya.shape[1])] + [full(a) for a in shared],
        out_specs=[row(d), row(d), row(LANES), pl.BlockSpec((1, n_exp), lambda i: (0, 0))],
        scratch_shapes=[pltpu.VMEM((1, n_exp), F32)],
        name="back_sample" if latent_values else "back_prompt", compiler_params=_params(1),
    )(h, ycn, ya, *shared)


def _dispatch_kernel(slot_ref, x_ref, xs_in_ref, xs_ref, sem):
    del xs_in_ref
    tm = x_ref.shape[0]

    def body(r, carry):
        for k in range(TOP_K):
            s = slot_ref[r * TOP_K + k]
            pltpu.make_async_copy(x_ref.at[pl.ds(r, 1)], xs_ref.at[pl.ds(s, 1)], sem).start(priority=k % 2)
        return carry

    lax.fori_loop(0, tm, body, 0, unroll=4)
    for k in range(TOP_K):
        pltpu.make_async_copy(x_ref, xs_ref.at[pl.ds(0, tm)], sem).wait()


def _dispatch(x, slots, xs):
    n, d = x.shape
    tm = min(DISPATCH_ROWS, n)
    return pl.pallas_call(
        _dispatch_kernel,
        out_shape=jax.ShapeDtypeStruct(xs.shape, F32),
        grid=(n // tm,),
        in_specs=[pl.BlockSpec((tm * TOP_K,), lambda i: (i,), memory_space=pltpu.SMEM),
                  pl.BlockSpec((tm, d), lambda i: (i, 0)),
                  pl.BlockSpec(memory_space=pl.ANY)],
        out_specs=pl.BlockSpec(memory_space=pl.ANY),
        scratch_shapes=[pltpu.SemaphoreType.DMA(())],
        input_output_aliases={2: 0},
        name="moe_dispatch", compiler_params=_params(1),
    )(slots, x, xs)


def _expert_kernel(te_ref, nu_ref, xs_ref, wgu_ref, bgu_ref, wd_ref, bd_ref, ys_ref, wgu_bf, wd_bf):
    ti = pl.program_id(0)
    used = ti < nu_ref[0]
    new_expert = (ti == 0) | (te_ref[ti] != te_ref[jnp.maximum(ti - 1, 0)])
    d_ff = wd_ref.shape[0]

    @pl.when(used & new_expert)
    def _():
        wgu_bf[...] = wgu_ref[...].astype(BF16)
        wd_bf[...] = wd_ref[...].astype(BF16)

    @pl.when(used)
    def _():
        x = xs_ref[...].astype(BF16)
        bgu = bgu_ref[...]
        gate = jnp.minimum(_dot(x, wgu_bf[:, :d_ff]) + bgu[:, :d_ff], SWIGLU_LIMIT)
        up = jnp.clip(_dot(x, wgu_bf[:, d_ff:]) + bgu[:, d_ff:], -SWIGLU_LIMIT, SWIGLU_LIMIT)
        hid = (up + 1.0) * gate * jax.nn.sigmoid(SWIGLU_ALPHA * gate)
        ys_ref[...] = _dot(hid.astype(BF16), wd_bf[...]) + bd_ref[...]

    @pl.when(jnp.logical_not(used))
    def _():
        ys_ref[...] = jnp.zeros_like(ys_ref)


def _experts(xs, tile_expert, n_used, w_gu, b_gu, w_d, b_d):
    n_slots, d = xs.shape
    n_exp, _, d_ff2 = w_gu.shape
    d_ff = d_ff2 // 2
    tmm = MOE_TILE
    rows = lambda t, te, nu: (jnp.minimum(t, nu[0] - 1), 0)
    grid_spec = pltpu.PrefetchScalarGridSpec(
        num_scalar_prefetch=2, grid=(n_slots // tmm,),
        in_specs=[pl.BlockSpec((tmm, d), rows),
                  pl.BlockSpec((None, d, d_ff2), lambda t, te, nu: (te[t], 0, 0)),
                  pl.BlockSpec((None, 1, d_ff2), lambda t, te, nu: (te[t], 0, 0)),
                  pl.BlockSpec((None, d_ff, d), lambda t, te, nu: (te[t], 0, 0)),
                  pl.BlockSpec((None, 1, d), lambda t, te, nu: (te[t], 0, 0))],
        out_specs=pl.BlockSpec((tmm, d), lambda t, te, nu: (t, 0)),
        scratch_shapes=[pltpu.VMEM((d, d_ff2), BF16), pltpu.VMEM((d_ff, d), BF16)])
    return pl.pallas_call(
        _expert_kernel, out_shape=jax.ShapeDtypeStruct((n_slots, d), F32), grid_spec=grid_spec,
        name="moe_experts", compiler_params=_params(1),
    )(tile_expert, n_used, xs, w_gu, b_gu.reshape(n_exp, 1, d_ff2), w_d, b_d.reshape(n_exp, 1, d))


def _combine_kernel(slot_ref, next_slot_ref, route_ref, hp_ref, ys_ref, gfin_ref, o_ref, buf, sem, *, steps):
    i = pl.program_id(0)
    tm = hp_ref.shape[0]

    def gather(slots, b):
        def body(r, carry):
            for k in range(TOP_K):
                s = slots[r * TOP_K + k]
                pltpu.make_async_copy(ys_ref.at[pl.ds(s, 1)], buf.at[b, k, pl.ds(r, 1)],
                                      sem.at[b]).start(priority=k % 2)
            return carry

        lax.fori_loop(0, tm, body, 0, unroll=4)

    @pl.when(i == 0)
    def _():
        gather(slot_ref, 0)

    if steps > 1:
        @pl.when(i + 1 < steps)
        def _():
            gather(next_slot_ref, (i + 1) % 2)

    b = i % 2
    for k in range(TOP_K):
        pltpu.make_async_copy(ys_ref.at[pl.ds(0, tm)], buf.at[b, k], sem.at[b]).wait()
    route = route_ref[...]
    acc = hp_ref[...]
    for k in range(TOP_K):
        acc = acc + route[:, ROUTE_GATE + k:ROUTE_GATE + k + 1] * buf[b, k]
    o_ref[...] = _rms(acc, gfin_ref[...])


def _combine(slots, route, hp, ys, g_fin):
    n, d = hp.shape
    tm = min(COMBINE_ROWS, n)
    steps = n // tm
    return pl.pallas_call(
        functools.partial(_combine_kernel, steps=steps),
        out_shape=jax.ShapeDtypeStruct((n, d), F32),
        grid=(steps,),
        in_specs=[pl.BlockSpec((tm * TOP_K,), lambda i: (i,), memory_space=pltpu.SMEM),
                  pl.BlockSpec((tm * TOP_K,), lambda i: (jnp.minimum(i + 1, steps - 1),), memory_space=pltpu.SMEM),
                  pl.BlockSpec((tm, LANES), lambda i: (i, 0)),
                  pl.BlockSpec((tm, d), lambda i: (i, 0)),
                  pl.BlockSpec(memory_space=pl.ANY),
                  pl.BlockSpec((1, d), lambda i: (0, 0))],
        out_specs=pl.BlockSpec((tm, d), lambda i: (i, 0)),
        scratch_shapes=[pltpu.VMEM((2, TOP_K, tm, d), F32), pltpu.SemaphoreType.DMA((2,))],
        name="moe_combine", compiler_params=_params(1),
    )(slots, slots, route, hp, ys, g_fin)


def _slot_tables(routes, counts, n_exp):
    tmm = MOE_TILE
    counts = [c.reshape(n_exp).astype(jnp.int32) for c in counts]
    total = sum(counts)
    group = ((total + tmm - 1) // tmm) * tmm
    ends = jnp.cumsum(group)
    start = ends - group
    slots = []
    before = jnp.zeros_like(total)
    experts = jnp.arange(n_exp, dtype=jnp.int32)
    for route, cnt in zip(routes, counts):
        eid = route[:, ROUTE_EXPERT:ROUTE_EXPERT + TOP_K].astype(jnp.int32)
        rank = route[:, ROUTE_RANK:ROUTE_RANK + TOP_K].astype(jnp.int32)
        first = jnp.sum(jnp.where(eid[:, :, None] == experts, start + before, 0), axis=-1)
        slots.append((first + rank).reshape(-1))
        before = before + cnt
    n_tokens = sum(r.shape[0] for r in routes)
    max_tiles = (n_tokens * TOP_K) // tmm + n_exp
    n_used = ends[-1] // tmm
    tile_start = jnp.minimum(jnp.arange(max_tiles, dtype=jnp.int32) * tmm, ends[-1] - 1)
    tile_expert = jnp.sum((ends[None, :] <= tile_start[:, None]).astype(jnp.int32), axis=1)
    return slots, tile_expert, n_used.reshape(1).astype(jnp.int32), max_tiles * tmm


def _prep_layer(norm_mix_g, w_in, conv_w, q_norm_g, kv_norm_g, w_uq, w_ukv, out_norm_conv_g, out_norm_attn_g,
                w_out, norm_ffn_g, w_router, b_router):
    d = w_in.shape[0]
    d_conv = conv_w.shape[1]
    q_lora = q_norm_g.shape[0]
    kv_lora = kv_norm_g.shape[0]
    c_kr = 3 * d_conv + q_lora + kv_lora
    w_kr = w_in[:, c_kr:c_kr + QK_ROPE]
    w_kr_rot = jnp.concatenate([-w_kr[:, HALF_ROPE:], w_kr[:, :HALF_ROPE]], axis=1)
    w_in_pad = jnp.concatenate([w_in[:, :c_kr], w_kr, w_kr_rot, jnp.zeros((d, LANES - 2 * QK_ROPE), F32)], axis=1)

    qk = QK_NOPE + QK_ROPE
    wq = w_uq.reshape(q_lora, N_HEADS, qk)
    pad = jnp.zeros((q_lora, N_HEADS, LANES - qk), F32)
    wq_pad = jnp.concatenate([wq, pad], axis=2)
    r1 = wq[:, :, QK_NOPE:QK_NOPE + HALF_ROPE]
    r2 = wq[:, :, QK_NOPE + HALF_ROPE:]
    wq_rot = jnp.concatenate([jnp.zeros((q_lora, N_HEADS, QK_NOPE), F32), -r2, r1, pad], axis=2)
    wq2 = jnp.concatenate([wq_pad.reshape(q_lora, -1), wq_rot.reshape(q_lora, -1)], axis=1)

    wkv = w_ukv.reshape(kv_lora, N_HEADS, QK_NOPE + V_DIM)
    w_uk = wkv[:, :, :QK_NOPE]
    w_uv = wkv[:, :, QK_NOPE:]
    wk_pad = jnp.concatenate([w_uk, jnp.zeros((kv_lora, N_HEADS, LANES - QK_NOPE), F32)], axis=2)
    wv_pad = jnp.concatenate([w_uv, jnp.zeros((kv_lora, N_HEADS, LANES - V_DIM), F32)], axis=2)
    v_one = jnp.zeros((1, N_HEADS, LANES), F32).at[:, :, V_DIM].set(1.0)
    place = jnp.zeros((QK_ROPE, N_HEADS, LANES), F32)
    place = place.at[jnp.arange(QK_ROPE), :, QK_NOPE + jnp.arange(QK_ROPE)].set(1.0)
    wuk_t = jnp.concatenate([jnp.transpose(w_uk, (1, 2, 0)),
                             jnp.zeros((N_HEADS, LANES - QK_NOPE, kv_lora), F32)], axis=1)
    eye = jnp.eye(N_HEADS, dtype=F32)
    w_uv_bd = jnp.einsum("chv,hg->hcgv", w_uv, eye).reshape(N_HEADS * kv_lora, N_HEADS * V_DIM)
    wr_hi = w_router.astype(BF16)
    wr_lo = (w_router - wr_hi.astype(F32)).astype(BF16)
    return {
        "d_conv": d_conv, "q_lora": q_lora, "kv_lora": kv_lora,
        "g_mix": norm_mix_g.reshape(1, -1), "w_in": w_in_pad.astype(BF16), "conv_w": conv_w,
        "g_q": q_norm_g.reshape(1, -1), "g_kv": kv_norm_g.reshape(1, -1), "wq2": wq2.astype(BF16),
        "g_conv": out_norm_conv_g.reshape(1, -1), "wk_pad": wk_pad.reshape(kv_lora, -1).astype(BF16),
        "place": place.reshape(QK_ROPE, -1).astype(BF16), "wv_pad": wv_pad.reshape(kv_lora, -1).astype(BF16),
        "v_one": v_one.reshape(1, -1), "wuk_t": wuk_t.astype(BF16),
        "g_attn": out_norm_attn_g.reshape(1, -1), "wo_conv": w_out[:d_conv, :].astype(BF16),
        "wo_attn": w_out[d_conv:, :].astype(BF16), "w_uv_bd": w_uv_bd.astype(BF16),
        "g_ffn": norm_ffn_g.reshape(1, -1), "wr_hi": wr_hi, "wr_lo": wr_lo, "b_router": b_router.reshape(1, -1),
    }


def kernel(x_prompt, x_sample, cache_kv_latent, cache_k_rope, state_conv, page_table, norm_mix_g, w_in, conv_w,
           q_norm_g, kv_norm_g, w_uq, w_ukv, out_norm_conv_g, out_norm_attn_g, w_out, norm_ffn_g, w_router,
           b_router, w_gate_up, b_gate_up, w_down, b_down, final_norm_g):
    depth = w_in.shape[0]
    batch, seq, d = x_prompt.shape
    dec_b, dec_t, _ = x_sample.shape
    n_exp = w_router.shape[-1]
    past_len = page_table.shape[1] * cache_kv_latent.shape[2]
    assert batch == 1, "the prompt path lays one sequence out along rows"
    assert depth == 1, "deeper stacks need the un-normalised residual between layers"
    pos_p = jnp.arange(seq, dtype=jnp.int32)
    pos_s = jnp.tile(past_len + jnp.arange(dec_t, dtype=jnp.int32), dec_b)
    hp = x_prompt.reshape(seq, d)
    hs = x_sample.reshape(dec_b * dec_t, d)
    g_fin = final_norm_g.reshape(1, -1)
    outs = [[] for _ in range(6)]
    for l in range(depth):
        w = _prep_layer(norm_mix_g[l], w_in[l], conv_w[l], q_norm_g[l], kv_norm_g[l], w_uq[l], w_ukv[l],
                        out_norm_conv_g[l], out_norm_attn_g[l], w_out[l], norm_ffn_g[l], w_router[l], b_router[l])
        d_conv, kv_lora = w["d_conv"], w["kv_lora"]

        u_p, ycn_p, q_p, ckv_p, kr_p, k_p, v_p = _front(hp, pos_p, w, prompt=True)
        ya_p = _flash(q_p, k_p, v_p)
        hp_mid, xn_p, route_p, count_p = _back(hp, ycn_p, ya_p, w, latent_values=False)

        st = state_conv[l]
        t_idx = jnp.arange(dec_t)
        ov1 = jnp.where((t_idx == 0)[None, :, None], st[:, 1:2, :], 0.0).reshape(dec_b * dec_t, d_conv)
        ov2 = jnp.where((t_idx == 0)[None, :, None], st[:, 0:1, :],
                        jnp.where((t_idx == 1)[None, :, None], st[:, 1:2, :], 0.0)).reshape(dec_b * dec_t, d_conv)
        u_s, ycn_s, q_s, ckv_s, kr_s, qlat_s = _front(hs, pos_s, w, prompt=False, ov=(ov1, ov2), seq_rows=dec_t)
        rows = dec_t * N_HEADS
        qr_s = q_s.reshape(dec_b, rows, LANES)[:, :, QK_NOPE:QK_NOPE + QK_ROPE]
        cache_krt = jnp.swapaxes(cache_k_rope[l], 1, 2)
        o_lat = _paged_attention(qlat_s.reshape(dec_b, rows, kv_lora), qr_s,
                                 ckv_s.reshape(dec_b, dec_t, kv_lora), kr_s.reshape(dec_b, dec_t, QK_ROPE),
                                 cache_kv_latent[l], cache_krt, page_table)
        ya_s = o_lat.reshape(dec_b * dec_t, N_HEADS * kv_lora)
        hs_mid, xn_s, route_s, count_s = _back(hs, ycn_s, ya_s, w, latent_values=True)

        (slot_p, slot_s), tile_expert, n_used, n_slots = _slot_tables((route_p, route_s), (count_p, count_s), n_exp)
        xs = jnp.zeros((n_slots, d), F32)
        xs = _dispatch(xn_p, slot_p, xs)
        xs = _dispatch(xn_s, slot_s, xs)
        ys = _experts(xs, tile_expert, n_used, w_gate_up[l], b_gate_up[l], w_down[l], b_down[l])
        hp = _combine(slot_p, route_p, hp_mid, ys, g_fin)
        hs = _combine(slot_s, route_s, hs_mid, ys, g_fin)

        outs[0].append(ckv_p.reshape(batch, seq, kv_lora))
        outs[1].append(kr_p.reshape(batch, seq, QK_ROPE))
        outs[2].append(u_p[seq - 2:, :].reshape(batch, 2, d_conv))
        outs[3].append(ckv_s.reshape(dec_b, dec_t, kv_lora))
        outs[4].append(kr_s.reshape(dec_b, dec_t, QK_ROPE))
        outs[5].append(u_s.reshape(dec_b, dec_t, d_conv)[:, dec_t - 2:, :])
    return (hp.reshape(batch, seq, d), hs.reshape(dec_b, dec_t, d)) + tuple(jnp.stack(o) for o in outs)
```

```python
import functools
import math

import jax
import jax.numpy as jnp
from jax import lax
from jax.experimental import pallas as pl
from jax.experimental.pallas import tpu as pltpu

N_HEADS = 8
QK_NOPE = 64
QK_ROPE = 32
V_DIM = 64
HALF_ROPE = QK_ROPE // 2
ROPE_THETA = 10000.0
SOFTMAX_SCALE = (QK_NOPE + QK_ROPE) ** -0.5
LOG2E = math.log2(math.e)
TOP_K = 4
SWIGLU_LIMIT = 7.0
SWIGLU_ALPHA = 1.702
RMS_EPS = 1e-6
NEG = -1e30

LANES = 128
MIB = 1024 * 1024
VMEM_LIMIT = 56 * MIB
F32 = jnp.float32
BF16 = jnp.bfloat16

FRONT_ROWS = 512
FLASH_ROWS = 1024
BACK_ROWS = 512
PAGES_PER_CHUNK = 64
MOE_TILE = 512
DISPATCH_ROWS = 512
COMBINE_ROWS = 512
ROUTE_EXPERT, ROUTE_RANK, ROUTE_GATE = 0, TOP_K, 2 * TOP_K

NT_DIMS = (((1,), (1,)), ((), ()))


def _rms(x, g):
    return x * lax.rsqrt(jnp.mean(x * x, axis=-1, keepdims=True) + RMS_EPS) * g


def _dot(a, b):
    return jnp.dot(a, b, preferred_element_type=F32)


def _dot_nt(a, b):
    return lax.dot_general(a, b, NT_DIMS, preferred_element_type=F32)


def _params(n_axes):
    return pltpu.CompilerParams(dimension_semantics=("arbitrary",) * n_axes, vmem_limit_bytes=VMEM_LIMIT)


def _front_kernel(*refs, d_conv, q_lora, kv_lora, seq_rows, prompt):
    if prompt:
        (x_ref, cs_ref, gmix_ref, win_ref, cw_ref, gq_ref, gkv_ref, wq_ref, gconv_ref,
         wk_ref, place_ref, wv_ref, vone_ref,
         u_ref, ycn_ref, q_ref, ckv_ref, kr_ref, k_ref, v_ref, carry_ref) = refs
    else:
        (x_ref, cs_ref, ov1_ref, ov2_ref, gmix_ref, win_ref, cw_ref, gq_ref, gkv_ref, wq_ref,
         gconv_ref, wukt_ref,
         u_ref, ycn_ref, q_ref, ckv_ref, kr_ref, qlat_ref) = refs
    tm = x_ref.shape[0]
    n = _rms(x_ref[...], gmix_ref[...]).astype(BF16)
    z = _dot(n, win_ref[...])
    c0 = 3 * d_conv
    b_g = z[:, 0:d_conv]
    u = z[:, d_conv:2 * d_conv] * z[:, 2 * d_conv:c0]
    z_q = z[:, c0:c0 + q_lora]
    z_kv = z[:, c0 + q_lora:c0 + q_lora + kv_lora]
    z_r = z[:, c0 + q_lora + kv_lora:]
    u_ref[...] = u

    row = lax.broadcasted_iota(jnp.int32, u.shape, 0)
    r1 = pltpu.roll(u, 1, 0)
    r2 = pltpu.roll(u, 2, 0)
    if prompt:
        @pl.when(pl.program_id(0) == 0)
        def _():
            carry_ref[...] = jnp.zeros_like(carry_ref)
        last1 = carry_ref[7:8, :]
        last2 = carry_ref[6:7, :]
        p1 = jnp.where(row == 0, last1, r1)
        p2 = jnp.where(row == 0, last2, jnp.where(row == 1, last1, r2))
        carry_ref[...] = u[tm - 8:tm, :]
    else:
        t = row % seq_rows
        p1 = jnp.where(t == 0, ov1_ref[...], r1)
        p2 = jnp.where(t < 2, ov2_ref[...], r2)
    cw = cw_ref[...]
    y_c = p2 * cw[0:1, :] + p1 * cw[1:2, :] + u * cw[2:3, :]
    ycn_ref[...] = _rms(b_g * y_c, gconv_ref[...]).astype(BF16)

    nq = _rms(z_q, gq_ref[...]).astype(BF16)
    q2 = _dot(nq, wq_ref[...])
    hw = N_HEADS * LANES
    cs = cs_ref[...]
    cos, sin = cs[:, :HALF_ROPE], cs[:, HALF_ROPE:]
    qs = SOFTMAX_SCALE * LOG2E
    pad = jnp.zeros((tm, LANES - QK_NOPE - QK_ROPE), F32)
    tqc = jnp.concatenate([jnp.full((tm, QK_NOPE), qs, F32), cos * qs, cos * qs, pad], axis=1)
    tqs = jnp.concatenate([jnp.zeros((tm, QK_NOPE), F32), sin * qs, sin * qs, pad], axis=1)
    q_heads = []
    for h in range(N_HEADS):
        qa = q2[:, h * LANES:(h + 1) * LANES]
        qb = q2[:, hw + h * LANES:hw + (h + 1) * LANES]
        q_heads.append((qa * tqc + qb * tqs).astype(BF16))
    q_ref[...] = jnp.concatenate(q_heads, axis=1)

    ckv = _rms(z_kv, gkv_ref[...])
    ckv_ref[...] = ckv
    tk = jnp.concatenate([cos, cos, sin, sin, jnp.zeros((tm, LANES - 2 * QK_ROPE), F32)], axis=1)
    zr = z_r * tk
    kr = zr[:, 0:QK_ROPE] + zr[:, QK_ROPE:2 * QK_ROPE]
    kr_ref[...] = kr

    if prompt:
        ckv_b = ckv.astype(BF16)
        k_ref[...] = (_dot(ckv_b, wk_ref[...]) + _dot(kr.astype(BF16), place_ref[...])).astype(BF16)
        v_ref[...] = (_dot(ckv_b, wv_ref[...]) + vone_ref[...]).astype(BF16)
    else:
        for h in range(N_HEADS):
            qlat_ref[:, h * kv_lora:(h + 1) * kv_lora] = _dot(q_heads[h], wukt_ref[h]).astype(BF16)


def _front(x, pos, w, *, prompt, ov=None, seq_rows=0):
    n, d = x.shape
    d_conv, q_lora, kv_lora = w["d_conv"], w["q_lora"], w["kv_lora"]
    tm = min(FRONT_ROWS, n)
    hw = N_HEADS * LANES
    inv_freq = ROPE_THETA ** (-jnp.arange(HALF_ROPE, dtype=F32) / HALF_ROPE)
    ang = pos.astype(F32)[:, None] * inv_freq[None, :]
    cs = jnp.concatenate([jnp.cos(ang), jnp.sin(ang)], axis=1)

    row = lambda c: pl.BlockSpec((tm, c), lambda i: (i, 0))
    full = lambda a: pl.BlockSpec(a.shape, lambda i: (0,) * a.ndim)
    ins = [x, cs]
    in_specs = [row(d), row(2 * HALF_ROPE)]
    if not prompt:
        ins += [ov[0], ov[1]]
        in_specs += [row(d_conv), row(d_conv)]
    shared = [w["g_mix"], w["w_in"], w["conv_w"], w["g_q"], w["g_kv"], w["wq2"], w["g_conv"]]
    shared += [w["wk_pad"], w["place"], w["wv_pad"], w["v_one"]] if prompt else [w["wuk_t"]]
    ins += shared
    in_specs += [full(a) for a in shared]

    outs = [jax.ShapeDtypeStruct((n, d_conv), F32), jax.ShapeDtypeStruct((n, d_conv), BF16),
            jax.ShapeDtypeStruct((n, hw), BF16), jax.ShapeDtypeStruct((n, kv_lora), F32),
            jax.ShapeDtypeStruct((n, QK_ROPE), F32)]
    out_specs = [row(d_conv), row(d_conv), row(hw), row(kv_lora), row(QK_ROPE)]
    scratch = []
    if prompt:
        outs += [jax.ShapeDtypeStruct((n, hw), BF16), jax.ShapeDtypeStruct((n, hw), BF16)]
        out_specs += [row(hw), row(hw)]
        scratch = [pltpu.VMEM((8, d_conv), F32)]
    else:
        outs += [jax.ShapeDtypeStruct((n, N_HEADS * kv_lora), BF16)]
        out_specs += [row(N_HEADS * kv_lora)]
    kern = functools.partial(_front_kernel, d_conv=d_conv, q_lora=q_lora, kv_lora=kv_lora, seq_rows=seq_rows,
                             prompt=prompt)
    return pl.pallas_call(
        kern, out_shape=outs, grid=(n // tm,), in_specs=in_specs, out_specs=out_specs, scratch_shapes=scratch,
        name="front_prompt" if prompt else "front_sample", compiler_params=_params(1),
    )(*ins)


def _flash_kernel(q_ref, k_ref, v_ref, o_ref, m_sc, acc_sc, *, tq):
    i = pl.program_id(1)
    m_sc[...] = jnp.full_like(m_sc, NEG)
    acc_sc[...] = jnp.zeros_like(acc_sc)
    half = tq // 2

    def block(row0, n_rows, key0, n_keys, diagonal):
        rows = slice(row0, row0 + n_rows)
        for h in range(2):
            sl = slice(h * LANES, (h + 1) * LANES)
            s = _dot_nt(q_ref[rows, sl], k_ref[pl.ds(key0, n_keys), sl])
            if diagonal:
                r = lax.broadcasted_iota(jnp.int32, s.shape, 0)
                c = lax.broadcasted_iota(jnp.int32, s.shape, 1)
                s = jnp.where(c <= r, s, NEG)
            m_prev = m_sc[h, rows]
            m_new = jnp.maximum(m_prev, jnp.max(s, axis=-1, keepdims=True))
            alpha = jnp.exp2(m_prev - m_new)
            p = jnp.exp2(s - jnp.concatenate([m_new] * (n_keys // LANES), axis=1))
            acc_sc[h, rows] = alpha * acc_sc[h, rows] + _dot(p.astype(BF16), v_ref[pl.ds(key0, n_keys), sl])
            m_sc[h, rows] = m_new

    def body(j, carry):
        block(0, tq, pl.multiple_of(j * tq, tq), tq, False)
        return carry

    lax.fori_loop(0, i, body, 0)
    diag = pl.multiple_of(i * tq, tq)
    block(0, tq, diag, half, True)
    block(half, half, diag + half, half, True)
    outs = []
    for h in range(2):
        a = acc_sc[h]
        outs.append(a[:, :V_DIM] / a[:, V_DIM:V_DIM + 1])
    o_ref[...] = jnp.concatenate(outs, axis=1)


def _flash(q, k, v):
    s = q.shape[0]
    tq = min(FLASH_ROWS, s)
    pairs = N_HEADS // 2
    return pl.pallas_call(
        functools.partial(_flash_kernel, tq=tq),
        out_shape=jax.ShapeDtypeStruct((s, N_HEADS * V_DIM), F32),
        grid=(pairs, s // tq),
        in_specs=[pl.BlockSpec((tq, 2 * LANES), lambda p, i: (i, p)),
                  pl.BlockSpec((s, 2 * LANES), lambda p, i: (0, p)),
                  pl.BlockSpec((s, 2 * LANES), lambda p, i: (0, p))],
        out_specs=pl.BlockSpec((tq, 2 * V_DIM), lambda p, i: (i, p)),
        scratch_shapes=[pltpu.VMEM((2, tq, LANES), F32), pltpu.VMEM((2, tq, LANES), F32)],
        name="flash_prompt", compiler_params=_params(2),
    )(q, k, v)


def _paged_kernel(pt_ref, qlat_ref, qr_ref, cnew_ref, krnew_ref, cache_c, cache_krt, o_ref, cbuf, krbuf, cbf, sem,
                  *, nb, n_pages, page, ppc, t_new):
    b = pl.program_id(0)
    slot = b % 2
    n_chunks = n_pages // ppc
    ck = ppc * page

    def fetch(bb, sl):
        for pg_i in range(n_pages):
            pg = pt_ref[bb, pg_i]
            ci, p = divmod(pg_i, ppc)
            pltpu.make_async_copy(cache_c.at[pg], cbuf.at[sl, pg_i], sem.at[sl]).start()
            pltpu.make_async_copy(cache_krt.at[pg], krbuf.at[sl, ci, :, pl.ds(p * page, page)], sem.at[sl]).start()

    @pl.when(b == 0)
    def _():
        fetch(0, 0)

    if nb > 1:
        @pl.when(b + 1 < nb)
        def _():
            fetch(b + 1, 1 - slot)

    pltpu.make_async_copy(cache_c.at[pl.ds(0, n_pages)], cbuf.at[slot], sem.at[slot]).wait()
    pltpu.make_async_copy(krbuf.at[1 - slot], krbuf.at[slot], sem.at[slot]).wait()

    qlat = qlat_ref[...]
    qr = qr_ref[...]
    rows, kv_lora = qlat.shape

    def scores(ci):
        cb = cbuf[slot, pl.ds(ci * ppc, ppc)].reshape(ck, kv_lora).astype(BF16)
        cbf[ci % 2] = cb
        return _dot_nt(qlat, cb) + _dot(qr, krbuf[slot, ci].astype(BF16))

    def absorb(ci, s, m_prev, l_prev, acc):
        m_new = jnp.maximum(m_prev, jnp.max(s, axis=-1, keepdims=True))
        alpha = jnp.exp2(m_prev - m_new)
        p_ = jnp.exp2(s - m_new)
        l_new = alpha * l_prev + jnp.sum(p_, axis=-1, keepdims=True)
        return m_new, l_new, alpha * acc + _dot(p_.astype(BF16), cbf[ci % 2])

    def chunk(ci, carry):
        s, m_prev, l_prev, acc = carry
        s_next = scores(ci + 1)
        return (s_next,) + absorb(ci, s, m_prev, l_prev, acc)

    init = (scores(0), jnp.full((rows, 1), NEG, F32), jnp.zeros((rows, 1), F32), jnp.zeros((rows, kv_lora), F32))
    s_last, m_old, l_old, acc = lax.fori_loop(0, n_chunks - 1, chunk, init)
    m_old, l_old, acc = absorb(n_chunks - 1, s_last, m_old, l_old, acc)

    ql = qlat.astype(F32)
    qrf = qr.astype(F32)
    cn = cnew_ref[...]
    kn = krnew_ref[...]
    t_row = lax.broadcasted_iota(jnp.int32, (rows, 1), 0) // N_HEADS
    s_new = []
    for j in range(t_new):
        sj = (jnp.sum(ql * cn[j:j + 1, :], axis=-1, keepdims=True)
              + jnp.sum(qrf * kn[j:j + 1, :], axis=-1, keepdims=True))
        s_new.append(jnp.where(t_row >= j, sj, NEG))
    m_fin = m_old
    for sj in s_new:
        m_fin = jnp.maximum(m_fin, sj)
    a = jnp.exp2(m_old - m_fin)
    l_fin = a * l_old
    acc = a * acc
    for j, sj in enumerate(s_new):
        pj = jnp.exp2(sj - m_fin)
        l_fin = l_fin + pj
        acc = acc + pj * cn[j:j + 1, :]
    o_ref[...] = acc / l_fin


def _paged_attention(qlat, qr, c_new, kr_new, cache_c, cache_krt, page_table):
    nb, rows, kv_lora = qlat.shape
    n_pages = page_table.shape[1]
    page = cache_c.shape[1]
    t_new = c_new.shape[1]
    ppc = math.gcd(PAGES_PER_CHUNK, n_pages)
    per_b = lambda r, w: pl.BlockSpec((None, r, w), lambda b, pt: (b, 0, 0))
    any_spec = pl.BlockSpec(memory_space=pl.ANY)
    grid_spec = pltpu.PrefetchScalarGridSpec(
        num_scalar_prefetch=1, grid=(nb,),
        in_specs=[per_b(rows, kv_lora), per_b(rows, QK_ROPE), per_b(t_new, kv_lora), per_b(t_new, QK_ROPE),
                  any_spec, any_spec],
        out_specs=per_b(rows, kv_lora),
        scratch_shapes=[pltpu.VMEM((2, n_pages, page, kv_lora), F32),
                        pltpu.VMEM((2, n_pages // ppc, QK_ROPE, ppc * page), F32),
                        pltpu.VMEM((2, ppc * page, kv_lora), BF16),
                        pltpu.SemaphoreType.DMA((2,))])
    return pl.pallas_call(
        functools.partial(_paged_kernel, nb=nb, n_pages=n_pages, page=page, ppc=ppc, t_new=t_new),
        out_shape=jax.ShapeDtypeStruct((nb, rows, kv_lora), F32), grid_spec=grid_spec, name="paged_sample",
        compiler_params=_params(1),
    )(page_table, qlat, qr, c_new, kr_new, cache_c, cache_krt)


def _back_kernel(*refs, latent_values):
    if latent_values:
        (h_ref, ycn_ref, ya_ref, wuv_ref, gattn_ref, woc_ref, woa_ref, gffn_ref, wrh_ref, wrl_ref, br_ref,
         hp_ref, xn_ref, route_ref, count_ref, base_sc) = refs
        ya = _dot(ya_ref[...].astype(BF16), wuv_ref[...])
    else:
        (h_ref, ycn_ref, ya_ref, gattn_ref, woc_ref, woa_ref, gffn_ref, wrh_ref, wrl_ref, br_ref,
         hp_ref, xn_ref, route_ref, count_ref, base_sc) = refs
        ya = ya_ref[...]
    yan = _rms(ya, gattn_ref[...]).astype(BF16)
    hp = h_ref[...] + _dot(ycn_ref[...], woc_ref[...]) + _dot(yan, woa_ref[...])
    hp_ref[...] = hp
    xn = _rms(hp, gffn_ref[...])
    xn_ref[...] = xn
    x_hi = xn.astype(BF16)
    x_lo = (xn - x_hi.astype(F32)).astype(BF16)
    logits = _dot(x_hi, wrh_ref[...]) + _dot(x_hi, wrl_ref[...]) + _dot(x_lo, wrh_ref[...]) + br_ref[...]
    tm, n_exp = logits.shape
    lane = lax.broadcasted_iota(jnp.int32, logits.shape, 1)
    work = logits
    picks = []
    for _ in range(TOP_K):
        vmax = jnp.max(work, axis=-1, keepdims=True)
        idx = jnp.min(jnp.where(work == vmax, lane, n_exp), axis=-1, keepdims=True)
        sel = lane == idx
        picks.append((vmax, idx, sel))
        work = jnp.where(sel, -jnp.inf, work)
    top = picks[0][0]
    exps = [jnp.exp(v - top) for v, _, _ in picks]
    denom = exps[0]
    for e in exps[1:]:
        denom = denom + e

    @pl.when(pl.program_id(0) == 0)
    def _():
        base_sc[...] = jnp.zeros_like(base_sc)
    onehot = jnp.zeros_like(logits)
    for _, _, sel in picks:
        onehot = jnp.where(sel, 1.0, onehot)
    r_i = lax.broadcasted_iota(jnp.int32, (tm, tm), 0)
    c_i = lax.broadcasted_iota(jnp.int32, (tm, tm), 1)
    earlier = jnp.where(c_i < r_i, 1.0, 0.0).astype(BF16)
    rank_all = _dot(earlier, onehot.astype(BF16)) + base_sc[...]
    base_sc[...] += jnp.sum(onehot, axis=0, keepdims=True)
    count_ref[...] = base_sc[...]

    lane_r = lax.broadcasted_iota(jnp.int32, (tm, LANES), 1)
    route = jnp.zeros((tm, LANES), F32)
    for k, (e, (_, idx, sel)) in enumerate(zip(exps, picks)):
        rank_k = jnp.sum(jnp.where(sel, rank_all, 0.0), axis=-1, keepdims=True)
        route = jnp.where(lane_r == ROUTE_EXPERT + k, idx.astype(F32), route)
        route = jnp.where(lane_r == ROUTE_RANK + k, rank_k, route)
        route = jnp.where(lane_r == ROUTE_GATE + k, e / denom, route)
    route_ref[...] = route


def _back(h, ycn, ya, w, *, latent_values):
    n, d = h.shape
    tm = min(BACK_ROWS, n)
    n_exp = w["wr_hi"].shape[1]
    row = lambda c: pl.BlockSpec((tm, c), lambda i: (i, 0))
    full = lambda a: pl.BlockSpec(a.shape, lambda i: (0,) * a.ndim)
    shared = [w["g_attn"], w["wo_conv"], w["wo_attn"], w["g_ffn"], w["wr_hi"], w["wr_lo"], w["b_router"]]
    if latent_values:
        shared = [w["w_uv_bd"]] + shared
    return pl.pallas_call(
        functools.partial(_back_kernel, latent_values=latent_values),
        out_shape=[jax.ShapeDtypeStruct((n, d), F32), jax.ShapeDtypeStruct((n, d), F32),
                   jax.ShapeDtypeStruct((n, LANES), F32), jax.ShapeDtypeStruct((1, n_exp), F32)],
        grid=(n // tm,),
        in_specs=[row(d), row(ycn.shape[1]), row(ya.shape[1])] + [full(a) for a in shared],
        out_specs=[row(d), row(d), row(LANES), pl.BlockSpec((1, n_exp), lambda i: (0, 0))],
        scratch_shapes=[pltpu.VMEM((1, n_exp), F32)],
        name="back_sample" if latent_values else "back_prompt", compiler_params=_params(1),
    )(h, ycn, ya, *shared)


def _dispatch_kernel(slot_ref, x_ref, xs_in_ref, xs_ref, sem):
    del xs_in_ref
    tm = x_ref.shape[0]

    def body(r, carry):
        for k in range(TOP_K):
            s = slot_ref[r * TOP_K + k]
            pltpu.make_async_copy(x_ref.at[pl.ds(r, 1)], xs_ref.at[pl.ds(s, 1)], sem).start(priority=k % 2)
        return carry

    lax.fori_loop(0, tm, body, 0, unroll=4)
    for k in range(TOP_K):
        pltpu.make_async_copy(x_ref, xs_ref.at[pl.ds(0, tm)], sem).wait()


def _dispatch(x, slots, xs):
    n, d = x.shape
    tm = min(DISPATCH_ROWS, n)
    return pl.pallas_call(
        _dispatch_kernel,
        out_shape=jax.ShapeDtypeStruct(xs.shape, F32),
        grid=(n // tm,),
        in_specs=[pl.BlockSpec((tm * TOP_K,), lambda i: (i,), memory_space=pltpu.SMEM),
                  pl.BlockSpec((tm, d), lambda i: (i, 0)),
                  pl.BlockSpec(memory_space=pl.ANY)],
        out_specs=pl.BlockSpec(memory_space=pl.ANY),
        scratch_shapes=[pltpu.SemaphoreType.DMA(())],
        input_output_aliases={2: 0},
        name="moe_dispatch", compiler_params=_params(1),
    )(slots, x, xs)


def _expert_kernel(te_ref, nx_ref, nu_ref, xs_ref, bgu_ref, bd_ref, wgu_hbm, wd_hbm, ys_ref,
                   wgu_st, wd_st, wgu_bf, wd_bf, sem):
    ti = pl.program_id(0)
    used = ti < nu_ref[0]
    new_expert = (ti == 0) | (te_ref[ti] != te_ref[jnp.maximum(ti - 1, 0)])
    d_ff = wd_bf.shape[0]

    def weight_copies(e):
        return (pltpu.make_async_copy(wgu_hbm.at[e], wgu_st, sem.at[0]),
                pltpu.make_async_copy(wd_hbm.at[e], wd_st, sem.at[1]))

    @pl.when(ti == 0)
    def _():
        for cp in weight_copies(te_ref[0]):
            cp.start()

    @pl.when(used & new_expert)
    def _():
        for cp in weight_copies(te_ref[ti]):
            cp.wait()
        wgu_bf[...] = wgu_st[...].astype(BF16)
        wd_bf[...] = wd_st[...].astype(BF16)

        @pl.when(nx_ref[ti] >= 0)
        def _():
            for cp in weight_copies(nx_ref[ti]):
                cp.start()

    @pl.when(used)
    def _():
        x = xs_ref[...].astype(BF16)
        bgu = bgu_ref[...]
        gate = jnp.minimum(_dot(x, wgu_bf[:, :d_ff]) + bgu[:, :d_ff], SWIGLU_LIMIT)
        up = jnp.clip(_dot(x, wgu_bf[:, d_ff:]) + bgu[:, d_ff:], -SWIGLU_LIMIT, SWIGLU_LIMIT)
        hid = (up + 1.0) * gate * jax.nn.sigmoid(SWIGLU_ALPHA * gate)
        ys_ref[...] = _dot(hid.astype(BF16), wd_bf[...]) + bd_ref[...]

    @pl.when(jnp.logical_not(used))
    def _():
        ys_ref[...] = jnp.zeros_like(ys_ref)


def _experts(xs, tile_expert, next_expert, n_used, w_gu, b_gu, w_d, b_d):
    n_slots, d = xs.shape
    n_exp, _, d_ff2 = w_gu.shape
    d_ff = d_ff2 // 2
    tmm = MOE_TILE
    rows = lambda t, te, nx, nu: (jnp.minimum(t, nu[0] - 1), 0)
    any_spec = pl.BlockSpec(memory_space=pl.ANY)
    grid_spec = pltpu.PrefetchScalarGridSpec(
        num_scalar_prefetch=3, grid=(n_slots // tmm,),
        in_specs=[pl.BlockSpec((tmm, d), rows),
                  pl.BlockSpec((None, 1, d_ff2), lambda t, te, nx, nu: (te[t], 0, 0)),
                  pl.BlockSpec((None, 1, d), lambda t, te, nx, nu: (te[t], 0, 0)),
                  any_spec, any_spec],
        out_specs=pl.BlockSpec((tmm, d), lambda t, te, nx, nu: (t, 0)),
        scratch_shapes=[pltpu.VMEM((d, d_ff2), F32), pltpu.VMEM((d_ff, d), F32),
                        pltpu.VMEM((d, d_ff2), BF16), pltpu.VMEM((d_ff, d), BF16),
                        pltpu.SemaphoreType.DMA((2,))])
    return pl.pallas_call(
        _expert_kernel, out_shape=jax.ShapeDtypeStruct((n_slots, d), F32), grid_spec=grid_spec,
        name="moe_experts", compiler_params=_params(1),
    )(tile_expert, next_expert, n_used, xs, b_gu.reshape(n_exp, 1, d_ff2), b_d.reshape(n_exp, 1, d), w_gu, w_d)


def _combine_kernel(slot_ref, next_slot_ref, route_ref, hp_ref, ys_ref, gfin_ref, o_ref, buf, sem, *, steps):
    i = pl.program_id(0)
    tm = hp_ref.shape[0]

    def gather(slots, b):
        def body(r, carry):
            for k in range(TOP_K):
                s = slots[r * TOP_K + k]
                pltpu.make_async_copy(ys_ref.at[pl.ds(s, 1)], buf.at[b, k, pl.ds(r, 1)],
                                      sem.at[b]).start(priority=k % 2)
            return carry

        lax.fori_loop(0, tm, body, 0, unroll=4)

    @pl.when(i == 0)
    def _():
        gather(slot_ref, 0)

    if steps > 1:
        @pl.when(i + 1 < steps)
        def _():
            gather(next_slot_ref, (i + 1) % 2)

    b = i % 2
    for k in range(TOP_K):
        pltpu.make_async_copy(ys_ref.at[pl.ds(0, tm)], buf.at[b, k], sem.at[b]).wait()
    route = route_ref[...]
    acc = hp_ref[...]
    for k in range(TOP_K):
        acc = acc + route[:, ROUTE_GATE + k:ROUTE_GATE + k + 1] * buf[b, k]
    o_ref[...] = _rms(acc, gfin_ref[...])


def _combine(slots, route, hp, ys, g_fin):
    n, d = hp.shape
    tm = min(COMBINE_ROWS, n)
    steps = n // tm
    return pl.pallas_call(
        functools.partial(_combine_kernel, steps=steps),
        out_shape=jax.ShapeDtypeStruct((n, d), F32),
        grid=(steps,),
        in_specs=[pl.BlockSpec((tm * TOP_K,), lambda i: (i,), memory_space=pltpu.SMEM),
                  pl.BlockSpec((tm * TOP_K,), lambda i: (jnp.minimum(i + 1, steps - 1),), memory_space=pltpu.SMEM),
                  pl.BlockSpec((tm, LANES), lambda i: (i, 0)),
                  pl.BlockSpec((tm, d), lambda i: (i, 0)),
                  pl.BlockSpec(memory_space=pl.ANY),
                  pl.BlockSpec((1, d), lambda i: (0, 0))],
        out_specs=pl.BlockSpec((tm, d), lambda i: (i, 0)),
        scratch_shapes=[pltpu.VMEM((2, TOP_K, tm, d), F32), pltpu.SemaphoreType.DMA((2,))],
        name="moe_combine", compiler_params=_params(1),
    )(slots, slots, route, hp, ys, g_fin)


def _slot_tables(routes, counts, n_exp):
    tmm = MOE_TILE
    counts = [c.reshape(n_exp).astype(jnp.int32) for c in counts]
    total = sum(counts)
    group = ((total + tmm - 1) // tmm) * tmm
    ends = jnp.cumsum(group)
    start = ends - group
    slots = []
    before = jnp.zeros_like(total)
    experts = jnp.arange(n_exp, dtype=jnp.int32)
    for route, cnt in zip(routes, counts):
        eid = route[:, ROUTE_EXPERT:ROUTE_EXPERT + TOP_K].astype(jnp.int32)
        rank = route[:, ROUTE_RANK:ROUTE_RANK + TOP_K].astype(jnp.int32)
        first = jnp.sum(jnp.where(eid[:, :, None] == experts, start + before, 0), axis=-1)
        slots.append((first + rank).reshape(-1))
        before = before + cnt
    n_tokens = sum(r.shape[0] for r in routes)
    max_tiles = (n_tokens * TOP_K) // tmm + n_exp
    n_used = ends[-1] // tmm
    tile_start = jnp.minimum(jnp.arange(max_tiles, dtype=jnp.int32) * tmm, ends[-1] - 1)
    tile_expert = jnp.sum((ends[None, :] <= tile_start[:, None]).astype(jnp.int32), axis=1)
    group_end = jnp.sum(jnp.where(tile_expert[:, None] == experts, ends, 0), axis=1)
    following = jnp.sum((ends[None, :] <= group_end[:, None]).astype(jnp.int32), axis=1)
    next_expert = jnp.where(group_end < ends[-1], following, -1).astype(jnp.int32)
    return slots, tile_expert, next_expert, n_used.reshape(1).astype(jnp.int32), max_tiles * tmm


def _prep_layer(norm_mix_g, w_in, conv_w, q_norm_g, kv_norm_g, w_uq, w_ukv, out_norm_conv_g, out_norm_attn_g,
                w_out, norm_ffn_g, w_router, b_router):
    d = w_in.shape[0]
    d_conv = conv_w.shape[1]
    q_lora = q_norm_g.shape[0]
    kv_lora = kv_norm_g.shape[0]
    c_kr = 3 * d_conv + q_lora + kv_lora
    w_kr = w_in[:, c_kr:c_kr + QK_ROPE]
    w_kr_rot = jnp.concatenate([-w_kr[:, HALF_ROPE:], w_kr[:, :HALF_ROPE]], axis=1)
    w_in_pad = jnp.concatenate([w_in[:, :c_kr], w_kr, w_kr_rot, jnp.zeros((d, LANES - 2 * QK_ROPE), F32)], axis=1)

    qk = QK_NOPE + QK_ROPE
    wq = w_uq.reshape(q_lora, N_HEADS, qk)
    pad = jnp.zeros((q_lora, N_HEADS, LANES - qk), F32)
    wq_pad = jnp.concatenate([wq, pad], axis=2)
    r1 = wq[:, :, QK_NOPE:QK_NOPE + HALF_ROPE]
    r2 = wq[:, :, QK_NOPE + HALF_ROPE:]
    wq_rot = jnp.concatenate([jnp.zeros((q_lora, N_HEADS, QK_NOPE), F32), -r2, r1, pad], axis=2)
    wq2 = jnp.concatenate([wq_pad.reshape(q_lora, -1), wq_rot.reshape(q_lora, -1)], axis=1)

    wkv = w_ukv.reshape(kv_lora, N_HEADS, QK_NOPE + V_DIM)
    w_uk = wkv[:, :, :QK_NOPE]
    w_uv = wkv[:, :, QK_NOPE:]
    wk_pad = jnp.concatenate([w_uk, jnp.zeros((kv_lora, N_HEADS, LANES - QK_NOPE), F32)], axis=2)
    wv_pad = jnp.concatenate([w_uv, jnp.zeros((kv_lora, N_HEADS, LANES - V_DIM), F32)], axis=2)
    v_one = jnp.zeros((1, N_HEADS, LANES), F32).at[:, :, V_DIM].set(1.0)
    place = jnp.zeros((QK_ROPE, N_HEADS, LANES), F32)
    place = place.at[jnp.arange(QK_ROPE), :, QK_NOPE + jnp.arange(QK_ROPE)].set(1.0)
    wuk_t = jnp.concatenate([jnp.transpose(w_uk, (1, 2, 0)),
                             jnp.zeros((N_HEADS, LANES - QK_NOPE, kv_lora), F32)], axis=1)
    eye = jnp.eye(N_HEADS, dtype=F32)
    w_uv_bd = jnp.einsum("chv,hg->hcgv", w_uv, eye).reshape(N_HEADS * kv_lora, N_HEADS * V_DIM)
    wr_hi = w_router.astype(BF16)
    wr_lo = (w_router - wr_hi.astype(F32)).astype(BF16)
    return {
        "d_conv": d_conv, "q_lora": q_lora, "kv_lora": kv_lora,
        "g_mix": norm_mix_g.reshape(1, -1), "w_in": w_in_pad.astype(BF16), "conv_w": conv_w,
        "g_q": q_norm_g.reshape(1, -1), "g_kv": kv_norm_g.reshape(1, -1), "wq2": wq2.astype(BF16),
        "g_conv": out_norm_conv_g.reshape(1, -1), "wk_pad": wk_pad.reshape(kv_lora, -1).astype(BF16),
        "place": place.reshape(QK_ROPE, -1).astype(BF16), "wv_pad": wv_pad.reshape(kv_lora, -1).astype(BF16),
        "v_one": v_one.reshape(1, -1), "wuk_t": wuk_t.astype(BF16),
        "g_attn": out_norm_attn_g.reshape(1, -1), "wo_conv": w_out[:d_conv, :].astype(BF16),
        "wo_attn": w_out[d_conv:, :].astype(BF16), "w_uv_bd": w_uv_bd.astype(BF16),
        "g_ffn": norm_ffn_g.reshape(1, -1), "wr_hi": wr_hi, "wr_lo": wr_lo, "b_router": b_router.reshape(1, -1),
    }


def kernel(x_prompt, x_sample, cache_kv_latent, cache_k_rope, state_conv, page_table, norm_mix_g, w_in, conv_w,
           q_norm_g, kv_norm_g, w_uq, w_ukv, out_norm_conv_g, out_norm_attn_g, w_out, norm_ffn_g, w_router,
           b_router, w_gate_up, b_gate_up, w_down, b_down, final_norm_g):
    depth = w_in.shape[0]
    batch, seq, d = x_prompt.shape
    dec_b, dec_t, _ = x_sample.shape
    n_exp = w_router.shape[-1]
    past_len = page_table.shape[1] * cache_kv_latent.shape[2]
    assert batch == 1, "the prompt path lays one sequence out along rows"
    assert depth == 1, "deeper stacks need the un-normalised residual between layers"
    pos_p = jnp.arange(seq, dtype=jnp.int32)
    pos_s = jnp.tile(past_len + jnp.arange(dec_t, dtype=jnp.int32), dec_b)
    hp = x_prompt.reshape(seq, d)
    hs = x_sample.reshape(dec_b * dec_t, d)
    g_fin = final_norm_g.reshape(1, -1)
    outs = [[] for _ in range(6)]
    for l in range(depth):
        w = _prep_layer(norm_mix_g[l], w_in[l], conv_w[l], q_norm_g[l], kv_norm_g[l], w_uq[l], w_ukv[l],
                        out_norm_conv_g[l], out_norm_attn_g[l], w_out[l], norm_ffn_g[l], w_router[l], b_router[l])
        d_conv, kv_lora = w["d_conv"], w["kv_lora"]

        u_p, ycn_p, q_p, ckv_p, kr_p, k_p, v_p = _front(hp, pos_p, w, prompt=True)
        ya_p = _flash(q_p, k_p, v_p)
        hp_mid, xn_p, route_p, count_p = _back(hp, ycn_p, ya_p, w, latent_values=False)

        st = state_conv[l]
        t_idx = jnp.arange(dec_t)
        ov1 = jnp.where((t_idx == 0)[None, :, None], st[:, 1:2, :], 0.0).reshape(dec_b * dec_t, d_conv)
        ov2 = jnp.where((t_idx == 0)[None, :, None], st[:, 0:1, :],
                        jnp.where((t_idx == 1)[None, :, None], st[:, 1:2, :], 0.0)).reshape(dec_b * dec_t, d_conv)
        u_s, ycn_s, q_s, ckv_s, kr_s, qlat_s = _front(hs, pos_s, w, prompt=False, ov=(ov1, ov2), seq_rows=dec_t)
        rows = dec_t * N_HEADS
        qr_s = q_s.reshape(dec_b, rows, LANES)[:, :, QK_NOPE:QK_NOPE + QK_ROPE]
        cache_krt = jnp.swapaxes(cache_k_rope[l], 1, 2)
        o_lat = _paged_attention(qlat_s.reshape(dec_b, rows, kv_lora), qr_s,
                                 ckv_s.reshape(dec_b, dec_t, kv_lora), kr_s.reshape(dec_b, dec_t, QK_ROPE),
                                 cache_kv_latent[l], cache_krt, page_table)
        ya_s = o_lat.reshape(dec_b * dec_t, N_HEADS * kv_lora)
        hs_mid, xn_s, route_s, count_s = _back(hs, ycn_s, ya_s, w, latent_values=True)

        (slot_p, slot_s), tile_expert, next_expert, n_used, n_slots = _slot_tables(
            (route_p, route_s), (count_p, count_s), n_exp)
        xs = jnp.zeros((n_slots, d), F32)
        xs = _dispatch(xn_p, slot_p, xs)
        xs = _dispatch(xn_s, slot_s, xs)
        ys = _experts(xs, tile_expert, next_expert, n_used, w_gate_up[l], b_gate_up[l], w_down[l], b_down[l])
        hp = _combine(slot_p, route_p, hp_mid, ys, g_fin)
        hs = _combine(slot_s, route_s, hs_mid, ys, g_fin)

        outs[0].append(ckv_p.reshape(batch, seq, kv_lora))
        outs[1].append(kr_p.reshape(batch, seq, QK_ROPE))
        outs[2].append(u_p[seq - 2:, :].reshape(batch, 2, d_conv))
        outs[3].append(ckv_s.reshape(dec_b, dec_t, kv_lora))
        outs[4].append(kr_s.reshape(dec_b, dec_t, QK_ROPE))
        outs[5].append(u_s.reshape(dec_b, dec_t, d_conv)[:, dec_t - 2:, :])
    return (hp.reshape(batch, seq, d), hs.reshape(dec_b, dec_t, d)) + tuple(jnp.stack(o) for o in outs)
```

```python
import functools
import math

import jax
import jax.numpy as jnp
from jax import lax
from jax.experimental import pallas as pl
from jax.experimental.pallas import tpu as pltpu

N_HEADS = 8
QK_NOPE = 64
QK_ROPE = 32
V_DIM = 64
HALF_ROPE = QK_ROPE // 2
ROPE_THETA = 10000.0
SOFTMAX_SCALE = (QK_NOPE + QK_ROPE) ** -0.5
LOG2E = math.log2(math.e)
TOP_K = 4
SWIGLU_LIMIT = 7.0
SWIGLU_ALPHA = 1.702
RMS_EPS = 1e-6
NEG = -1e30

LANES = 128
MIB = 1024 * 1024
VMEM_LIMIT = 56 * MIB
F32 = jnp.float32
BF16 = jnp.bfloat16

FRONT_ROWS = 512
FLASH_ROWS = 1024
BACK_ROWS = 1024
BACK_SUB_ROWS = 512
PAGES_PER_CHUNK = 64
MOE_TILE = 512
DISPATCH_ROWS = 512
COMBINE_ROWS = 512
ROUTE_EXPERT, ROUTE_RANK, ROUTE_GATE = 0, TOP_K, 2 * TOP_K

NT_DIMS = (((1,), (1,)), ((), ()))


def _rms(x, g):
    return x * lax.rsqrt(jnp.mean(x * x, axis=-1, keepdims=True) + RMS_EPS) * g


def _dot(a, b):
    return jnp.dot(a, b, preferred_element_type=F32)


def _dot_nt(a, b):
    return lax.dot_general(a, b, NT_DIMS, preferred_element_type=F32)


def _params(n_axes):
    return pltpu.CompilerParams(dimension_semantics=("arbitrary",) * n_axes, vmem_limit_bytes=VMEM_LIMIT)


def _front_kernel(*refs, d_conv, q_lora, kv_lora, seq_rows, prompt):
    if prompt:
        (x_ref, cs_ref, gmix_ref, win_ref, cw_ref, gq_ref, gkv_ref, wq_ref, gconv_ref,
         wk_ref, place_ref, wv_ref, vone_ref,
         u_ref, ycn_ref, q_ref, ckv_ref, kr_ref, k_ref, v_ref, carry_ref) = refs
    else:
        (x_ref, cs_ref, ov1_ref, ov2_ref, gmix_ref, win_ref, cw_ref, gq_ref, gkv_ref, wq_ref,
         gconv_ref, wukt_ref,
         u_ref, ycn_ref, q_ref, ckv_ref, kr_ref, qlat_ref) = refs
    tm = x_ref.shape[0]
    n = _rms(x_ref[...], gmix_ref[...]).astype(BF16)
    z = _dot(n, win_ref[...])
    c0 = 3 * d_conv
    b_g = z[:, 0:d_conv]
    u = z[:, d_conv:2 * d_conv] * z[:, 2 * d_conv:c0]
    z_q = z[:, c0:c0 + q_lora]
    z_kv = z[:, c0 + q_lora:c0 + q_lora + kv_lora]
    z_r = z[:, c0 + q_lora + kv_lora:]
    u_ref[...] = u

    row = lax.broadcasted_iota(jnp.int32, u.shape, 0)
    r1 = pltpu.roll(u, 1, 0)
    r2 = pltpu.roll(u, 2, 0)
    if prompt:
        @pl.when(pl.program_id(0) == 0)
        def _():
            carry_ref[...] = jnp.zeros_like(carry_ref)
        last1 = carry_ref[7:8, :]
        last2 = carry_ref[6:7, :]
        p1 = jnp.where(row == 0, last1, r1)
        p2 = jnp.where(row == 0, last2, jnp.where(row == 1, last1, r2))
        carry_ref[...] = u[tm - 8:tm, :]
    else:
        t = row % seq_rows
        p1 = jnp.where(t == 0, ov1_ref[...], r1)
        p2 = jnp.where(t < 2, ov2_ref[...], r2)
    cw = cw_ref[...]
    y_c = p2 * cw[0:1, :] + p1 * cw[1:2, :] + u * cw[2:3, :]
    ycn_ref[...] = _rms(b_g * y_c, gconv_ref[...]).astype(BF16)

    nq = _rms(z_q, gq_ref[...]).astype(BF16)
    q2 = _dot(nq, wq_ref[...])
    hw = N_HEADS * LANES
    cs = cs_ref[...]
    cos, sin = cs[:, :HALF_ROPE], cs[:, HALF_ROPE:]
    qs = SOFTMAX_SCALE * LOG2E
    pad = jnp.zeros((tm, LANES - QK_NOPE - QK_ROPE), F32)
    tqc = jnp.concatenate([jnp.full((tm, QK_NOPE), qs, F32), cos * qs, cos * qs, pad], axis=1)
    tqs = jnp.concatenate([jnp.zeros((tm, QK_NOPE), F32), sin * qs, sin * qs, pad], axis=1)
    q_heads = []
    for h in range(N_HEADS):
        qa = q2[:, h * LANES:(h + 1) * LANES]
        qb = q2[:, hw + h * LANES:hw + (h + 1) * LANES]
        q_heads.append((qa * tqc + qb * tqs).astype(BF16))
    q_ref[...] = jnp.concatenate(q_heads, axis=1)

    ckv = _rms(z_kv, gkv_ref[...])
    ckv_ref[...] = ckv
    tk = jnp.concatenate([cos, cos, sin, sin, jnp.zeros((tm, LANES - 2 * QK_ROPE), F32)], axis=1)
    zr = z_r * tk
    kr = zr[:, 0:QK_ROPE] + zr[:, QK_ROPE:2 * QK_ROPE]
    kr_ref[...] = kr

    if prompt:
        ckv_b = ckv.astype(BF16)
        k_ref[...] = (_dot(ckv_b, wk_ref[...]) + _dot(kr.astype(BF16), place_ref[...])).astype(BF16)
        v_ref[...] = (_dot(ckv_b, wv_ref[...]) + vone_ref[...]).astype(BF16)
    else:
        for h in range(N_HEADS):
            qlat_ref[:, h * kv_lora:(h + 1) * kv_lora] = _dot(q_heads[h], wukt_ref[h]).astype(BF16)


def _front(x, pos, w, *, prompt, ov=None, seq_rows=0):
    n, d = x.shape
    d_conv, q_lora, kv_lora = w["d_conv"], w["q_lora"], w["kv_lora"]
    tm = min(FRONT_ROWS, n)
    hw = N_HEADS * LANES
    inv_freq = ROPE_THETA ** (-jnp.arange(HALF_ROPE, dtype=F32) / HALF_ROPE)
    ang = pos.astype(F32)[:, None] * inv_freq[None, :]
    cs = jnp.concatenate([jnp.cos(ang), jnp.sin(ang)], axis=1)

    row = lambda c: pl.BlockSpec((tm, c), lambda i: (i, 0))
    full = lambda a: pl.BlockSpec(a.shape, lambda i: (0,) * a.ndim)
    ins = [x, cs]
    in_specs = [row(d), row(2 * HALF_ROPE)]
    if not prompt:
        ins += [ov[0], ov[1]]
        in_specs += [row(d_conv), row(d_conv)]
    shared = [w["g_mix"], w["w_in"], w["conv_w"], w["g_q"], w["g_kv"], w["wq2"], w["g_conv"]]
    shared += [w["wk_pad"], w["place"], w["wv_pad"], w["v_one"]] if prompt else [w["wuk_t"]]
    ins += shared
    in_specs += [full(a) for a in shared]

    outs = [jax.ShapeDtypeStruct((n, d_conv), F32), jax.ShapeDtypeStruct((n, d_conv), BF16),
            jax.ShapeDtypeStruct((n, hw), BF16), jax.ShapeDtypeStruct((n, kv_lora), F32),
            jax.ShapeDtypeStruct((n, QK_ROPE), F32)]
    out_specs = [row(d_conv), row(d_conv), row(hw), row(kv_lora), row(QK_ROPE)]
    scratch = []
    if prompt:
        outs += [jax.ShapeDtypeStruct((n, hw), BF16), jax.ShapeDtypeStruct((n, hw), BF16)]
        out_specs += [row(hw), row(hw)]
        scratch = [pltpu.VMEM((8, d_conv), F32)]
    else:
        outs += [jax.ShapeDtypeStruct((n, N_HEADS * kv_lora), BF16)]
        out_specs += [row(N_HEADS * kv_lora)]
    kern = functools.partial(_front_kernel, d_conv=d_conv, q_lora=q_lora, kv_lora=kv_lora, seq_rows=seq_rows,
                             prompt=prompt)
    return pl.pallas_call(
        kern, out_shape=outs, grid=(n // tm,), in_specs=in_specs, out_specs=out_specs, scratch_shapes=scratch,
        name="front_prompt" if prompt else "front_sample", compiler_params=_params(1),
    )(*ins)


def _flash_kernel(q_ref, k_ref, v_ref, o_ref, m_sc, acc_sc, *, tq):
    i = pl.program_id(1)
    m_sc[...] = jnp.full_like(m_sc, NEG)
    acc_sc[...] = jnp.zeros_like(acc_sc)
    half = tq // 2

    def block(row0, n_rows, key0, n_keys, diagonal):
        rows = slice(row0, row0 + n_rows)
        for h in range(2):
            sl = slice(h * LANES, (h + 1) * LANES)
            s = _dot_nt(q_ref[rows, sl], k_ref[pl.ds(key0, n_keys), sl])
            if diagonal:
                r = lax.broadcasted_iota(jnp.int32, s.shape, 0)
                c = lax.broadcasted_iota(jnp.int32, s.shape, 1)
                s = jnp.where(c <= r, s, NEG)
            m_prev = m_sc[h, rows]
            m_new = jnp.maximum(m_prev, jnp.max(s, axis=-1, keepdims=True))
            alpha = jnp.exp2(m_prev - m_new)
            p = jnp.exp2(s - jnp.concatenate([m_new] * (n_keys // LANES), axis=1))
            acc_sc[h, rows] = alpha * acc_sc[h, rows] + _dot(p.astype(BF16), v_ref[pl.ds(key0, n_keys), sl])
            m_sc[h, rows] = m_new

    def body(j, carry):
        block(0, tq, pl.multiple_of(j * tq, tq), tq, False)
        return carry

    lax.fori_loop(0, i, body, 0)
    diag = pl.multiple_of(i * tq, tq)
    block(0, tq, diag, half, True)
    block(half, half, diag + half, half, True)
    outs = []
    for h in range(2):
        a = acc_sc[h]
        outs.append(a[:, :V_DIM] / a[:, V_DIM:V_DIM + 1])
    o_ref[...] = jnp.concatenate(outs, axis=1)


def _flash(q, k, v):
    s = q.shape[0]
    tq = min(FLASH_ROWS, s)
    pairs = N_HEADS // 2
    return pl.pallas_call(
        functools.partial(_flash_kernel, tq=tq),
        out_shape=jax.ShapeDtypeStruct((s, N_HEADS * V_DIM), F32),
        grid=(pairs, s // tq),
        in_specs=[pl.BlockSpec((tq, 2 * LANES), lambda p, i: (i, p)),
                  pl.BlockSpec((s, 2 * LANES), lambda p, i: (0, p)),
                  pl.BlockSpec((s, 2 * LANES), lambda p, i: (0, p))],
        out_specs=pl.BlockSpec((tq, 2 * V_DIM), lambda p, i: (i, p)),
        scratch_shapes=[pltpu.VMEM((2, tq, LANES), F32), pltpu.VMEM((2, tq, LANES), F32)],
        name="flash_prompt", compiler_params=_params(2),
    )(q, k, v)


def _paged_kernel(pt_ref, qlat_ref, qr_ref, cnew_ref, krnew_ref, cache_c, cache_krt, o_ref, cbuf, krbuf, cbf, sem,
                  *, nb, n_pages, page, ppc, t_new):
    b = pl.program_id(0)
    slot = b % 2
    n_chunks = n_pages // ppc
    ck = ppc * page

    def fetch(bb, sl):
        for pg_i in range(n_pages):
            pg = pt_ref[bb, pg_i]
            ci, p = divmod(pg_i, ppc)
            pltpu.make_async_copy(cache_c.at[pg], cbuf.at[sl, pg_i], sem.at[sl]).start()
            pltpu.make_async_copy(cache_krt.at[pg], krbuf.at[sl, ci, :, pl.ds(p * page, page)], sem.at[sl]).start()

    @pl.when(b == 0)
    def _():
        fetch(0, 0)

    if nb > 1:
        @pl.when(b + 1 < nb)
        def _():
            fetch(b + 1, 1 - slot)

    pltpu.make_async_copy(cache_c.at[pl.ds(0, n_pages)], cbuf.at[slot], sem.at[slot]).wait()
    pltpu.make_async_copy(krbuf.at[1 - slot], krbuf.at[slot], sem.at[slot]).wait()

    qlat = qlat_ref[...]
    qr = qr_ref[...]
    rows, kv_lora = qlat.shape

    def scores(ci):
        cb = cbuf[slot, pl.ds(ci * ppc, ppc)].reshape(ck, kv_lora).astype(BF16)
        cbf[ci % 2] = cb
        return _dot_nt(qlat, cb) + _dot(qr, krbuf[slot, ci].astype(BF16))

    def absorb(ci, s, m_prev, l_prev, acc):
        m_new = jnp.maximum(m_prev, jnp.max(s, axis=-1, keepdims=True))
        alpha = jnp.exp2(m_prev - m_new)
        p_ = jnp.exp2(s - m_new)
        l_new = alpha * l_prev + jnp.sum(p_, axis=-1, keepdims=True)
        return m_new, l_new, alpha * acc + _dot(p_.astype(BF16), cbf[ci % 2])

    def chunk(ci, carry):
        s, m_prev, l_prev, acc = carry
        s_next = scores(ci + 1)
        return (s_next,) + absorb(ci, s, m_prev, l_prev, acc)

    init = (scores(0), jnp.full((rows, 1), NEG, F32), jnp.zeros((rows, 1), F32), jnp.zeros((rows, kv_lora), F32))
    s_last, m_old, l_old, acc = lax.fori_loop(0, n_chunks - 1, chunk, init)
    m_old, l_old, acc = absorb(n_chunks - 1, s_last, m_old, l_old, acc)

    ql = qlat.astype(F32)
    qrf = qr.astype(F32)
    cn = cnew_ref[...]
    kn = krnew_ref[...]
    t_row = lax.broadcasted_iota(jnp.int32, (rows, 1), 0) // N_HEADS
    s_new = []
    for j in range(t_new):
        sj = (jnp.sum(ql * cn[j:j + 1, :], axis=-1, keepdims=True)
              + jnp.sum(qrf * kn[j:j + 1, :], axis=-1, keepdims=True))
        s_new.append(jnp.where(t_row >= j, sj, NEG))
    m_fin = m_old
    for sj in s_new:
        m_fin = jnp.maximum(m_fin, sj)
    a = jnp.exp2(m_old - m_fin)
    l_fin = a * l_old
    acc = a * acc
    for j, sj in enumerate(s_new):
        pj = jnp.exp2(sj - m_fin)
        l_fin = l_fin + pj
        acc = acc + pj * cn[j:j + 1, :]
    o_ref[...] = acc / l_fin


def _paged_attention(qlat, qr, c_new, kr_new, cache_c, cache_krt, page_table):
    nb, rows, kv_lora = qlat.shape
    n_pages = page_table.shape[1]
    page = cache_c.shape[1]
    t_new = c_new.shape[1]
    ppc = math.gcd(PAGES_PER_CHUNK, n_pages)
    per_b = lambda r, w: pl.BlockSpec((None, r, w), lambda b, pt: (b, 0, 0))
    any_spec = pl.BlockSpec(memory_space=pl.ANY)
    grid_spec = pltpu.PrefetchScalarGridSpec(
        num_scalar_prefetch=1, grid=(nb,),
        in_specs=[per_b(rows, kv_lora), per_b(rows, QK_ROPE), per_b(t_new, kv_lora), per_b(t_new, QK_ROPE),
                  any_spec, any_spec],
        out_specs=per_b(rows, kv_lora),
        scratch_shapes=[pltpu.VMEM((2, n_pages, page, kv_lora), F32),
                        pltpu.VMEM((2, n_pages // ppc, QK_ROPE, ppc * page), F32),
                        pltpu.VMEM((2, ppc * page, kv_lora), BF16),
                        pltpu.SemaphoreType.DMA((2,))])
    return pl.pallas_call(
        functools.partial(_paged_kernel, nb=nb, n_pages=n_pages, page=page, ppc=ppc, t_new=t_new),
        out_shape=jax.ShapeDtypeStruct((nb, rows, kv_lora), F32), grid_spec=grid_spec, name="paged_sample",
        compiler_params=_params(1),
    )(page_table, qlat, qr, c_new, kr_new, cache_c, cache_krt)


def _back_kernel(*refs, latent_values, sub):
    if latent_values:
        (h_ref, ycn_ref, ya_ref, wuv_ref, gattn_ref, woc_ref, woa_ref, gffn_ref, wrh_ref, wrl_ref, br_ref, tri_ref,
         hp_ref, xn_ref, route_ref, count_ref, base_sc) = refs
    else:
        (h_ref, ycn_ref, ya_ref, gattn_ref, woc_ref, woa_ref, gffn_ref, wrh_ref, wrl_ref, br_ref, tri_ref,
         hp_ref, xn_ref, route_ref, count_ref, base_sc) = refs

    @pl.when(pl.program_id(0) == 0)
    def _():
        base_sc[...] = jnp.zeros_like(base_sc)

    def sub_tile(rows, base):
        if latent_values:
            ya = _dot(ya_ref[rows, :].astype(BF16), wuv_ref[...])
        else:
            ya = ya_ref[rows, :]
        yan = _rms(ya, gattn_ref[...]).astype(BF16)
        hp = h_ref[rows, :] + _dot(ycn_ref[rows, :], woc_ref[...]) + _dot(yan, woa_ref[...])
        hp_ref[rows, :] = hp
        xn = _rms(hp, gffn_ref[...])
        xn_ref[rows, :] = xn
        x_hi = xn.astype(BF16)
        x_lo = (xn - x_hi.astype(F32)).astype(BF16)
        logits = _dot(x_hi, wrh_ref[...]) + _dot(x_hi, wrl_ref[...]) + _dot(x_lo, wrh_ref[...]) + br_ref[...]
        n_exp = logits.shape[1]
        lane = lax.broadcasted_iota(jnp.int32, logits.shape, 1).astype(F32)
        work = logits
        picks = []
        for _ in range(TOP_K):
            vmax = jnp.max(work, axis=-1, keepdims=True)
            idx = jnp.min(jnp.where(work == vmax, lane, float(n_exp)), axis=-1, keepdims=True)
            sel = lane == idx
            picks.append((vmax, idx, sel))
            work = jnp.where(sel, -jnp.inf, work)
        top = picks[0][0]
        exps = [jnp.exp(v - top) for v, _, _ in picks]
        denom = exps[0]
        for e in exps[1:]:
            denom = denom + e

        onehot = jnp.zeros_like(logits)
        for _, _, sel in picks:
            onehot = jnp.where(sel, 1.0, onehot)
        rank_all = _dot(tri_ref[...], onehot.astype(BF16)) + base

        lane_r = lax.broadcasted_iota(jnp.int32, (sub, LANES), 1)
        route = jnp.zeros((sub, LANES), F32)
        for k, (e, (_, idx, sel)) in enumerate(zip(exps, picks)):
            rank_k = jnp.sum(jnp.where(sel, rank_all, 0.0), axis=-1, keepdims=True)
            route = jnp.where(lane_r == ROUTE_EXPERT + k, idx, route)
            route = jnp.where(lane_r == ROUTE_RANK + k, rank_k, route)
            route = jnp.where(lane_r == ROUTE_GATE + k, e / denom, route)
        route_ref[rows, :] = route
        return base + jnp.sum(onehot, axis=0, keepdims=True)

    base = base_sc[...]
    for s in range(h_ref.shape[0] // sub):
        base = sub_tile(slice(s * sub, (s + 1) * sub), base)
    base_sc[...] = base
    count_ref[...] = base


def _back(h, ycn, ya, w, *, latent_values):
    n, d = h.shape
    tm = min(BACK_ROWS, n)
    sub = min(BACK_SUB_ROWS, tm)
    n_exp = w["wr_hi"].shape[1]
    row = lambda c: pl.BlockSpec((tm, c), lambda i: (i, 0))
    full = lambda a: pl.BlockSpec(a.shape, lambda i: (0,) * a.ndim)
    tri = jnp.tril(jnp.ones((sub, sub), F32), k=-1).astype(BF16)
    shared = [w["g_attn"], w["wo_conv"], w["wo_attn"], w["g_ffn"], w["wr_hi"], w["wr_lo"], w["b_router"], tri]
    if latent_values:
        shared = [w["w_uv_bd"]] + shared
    return pl.pallas_call(
        functools.partial(_back_kernel, latent_values=latent_values, sub=sub),
        out_shape=[jax.ShapeDtypeStruct((n, d), F32), jax.ShapeDtypeStruct((n, d), F32),
                   jax.ShapeDtypeStruct((n, LANES), F32), jax.ShapeDtypeStruct((1, n_exp), F32)],
        grid=(n // tm,),
        in_specs=[row(d), row(ycn.shape[1]), row(ya.shape[1])] + [full(a) for a in shared],
        out_specs=[row(d), row(d), row(LANES), pl.BlockSpec((1, n_exp), lambda i: (0, 0))],
        scratch_shapes=[pltpu.VMEM((1, n_exp), F32)],
        name="back_sample" if latent_values else "back_prompt", compiler_params=_params(1),
    )(h, ycn, ya, *shared)


def _dispatch_kernel(slot_ref, x_ref, xs_in_ref, xs_ref, sem):
    del xs_in_ref
    tm = x_ref.shape[0]

    def body(r, carry):
        for k in range(TOP_K):
            s = slot_ref[r * TOP_K + k]
            pltpu.make_async_copy(x_ref.at[pl.ds(r, 1)], xs_ref.at[pl.ds(s, 1)], sem).start(priority=k % 2)
        return carry

    lax.fori_loop(0, tm, body, 0, unroll=4)
    for k in range(TOP_K):
        pltpu.make_async_copy(x_ref, xs_ref.at[pl.ds(0, tm)], sem).wait()


def _dispatch(x, slots, xs):
    n, d = x.shape
    tm = min(DISPATCH_ROWS, n)
    return pl.pallas_call(
        _dispatch_kernel,
        out_shape=jax.ShapeDtypeStruct(xs.shape, F32),
        grid=(n // tm,),
        in_specs=[pl.BlockSpec((tm * TOP_K,), lambda i: (i,), memory_space=pltpu.SMEM),
                  pl.BlockSpec((tm, d), lambda i: (i, 0)),
                  pl.BlockSpec(memory_space=pl.ANY)],
        out_specs=pl.BlockSpec(memory_space=pl.ANY),
        scratch_shapes=[pltpu.SemaphoreType.DMA(())],
        input_output_aliases={2: 0},
        name="moe_dispatch", compiler_params=_params(1),
    )(slots, x, xs)


def _expert_kernel(te_ref, nx_ref, vr_ref, nu_ref, xs_ref, bgu_ref, bd_ref, wgu_hbm, wd_hbm, ys_ref,
                   wgu_st, wd_st, wgu_bf, wd_bf, sem):
    ti = pl.program_id(0)
    used = ti < nu_ref[0]
    new_expert = (ti == 0) | (te_ref[ti] != te_ref[jnp.maximum(ti - 1, 0)])
    d_ff = wd_bf.shape[0]

    def weight_copies(e):
        return (pltpu.make_async_copy(wgu_hbm.at[e], wgu_st, sem.at[0]),
                pltpu.make_async_copy(wd_hbm.at[e], wd_st, sem.at[1]))

    @pl.when(ti == 0)
    def _():
        for cp in weight_copies(te_ref[0]):
            cp.start()

    @pl.when(used & new_expert)
    def _():
        for cp in weight_copies(te_ref[ti]):
            cp.wait()
        wgu_bf[...] = wgu_st[...].astype(BF16)
        wd_bf[...] = wd_st[...].astype(BF16)

        @pl.when(nx_ref[ti] >= 0)
        def _():
            for cp in weight_copies(nx_ref[ti]):
                cp.start()

    def mlp(x):
        x = x.astype(BF16)
        bgu = bgu_ref[...]
        gate = jnp.minimum(_dot(x, wgu_bf[:, :d_ff]) + bgu[:, :d_ff], SWIGLU_LIMIT)
        up = jnp.clip(_dot(x, wgu_bf[:, d_ff:]) + bgu[:, d_ff:], -SWIGLU_LIMIT, SWIGLU_LIMIT)
        hid = (up + 1.0) * gate * jax.nn.sigmoid(SWIGLU_ALPHA * gate)
        return _dot(hid.astype(BF16), wd_bf[...]) + bd_ref[...]

    half = xs_ref.shape[0] // 2
    valid = vr_ref[ti]

    @pl.when(used & (valid > half))
    def _():
        ys_ref[...] = mlp(xs_ref[...])

    @pl.when(used & (valid <= half))
    def _():
        ys_ref[:half, :] = mlp(xs_ref[:half, :])
        ys_ref[half:, :] = jnp.zeros((half, ys_ref.shape[1]), F32)

    @pl.when(jnp.logical_not(used))
    def _():
        ys_ref[...] = jnp.zeros_like(ys_ref)


def _experts(xs, tile_expert, next_expert, valid_rows, n_used, w_gu, b_gu, w_d, b_d):
    n_slots, d = xs.shape
    n_exp, _, d_ff2 = w_gu.shape
    d_ff = d_ff2 // 2
    tmm = MOE_TILE
    rows = lambda t, te, nx, vr, nu: (jnp.minimum(t, nu[0] - 1), 0)
    any_spec = pl.BlockSpec(memory_space=pl.ANY)
    grid_spec = pltpu.PrefetchScalarGridSpec(
        num_scalar_prefetch=4, grid=(n_slots // tmm,),
        in_specs=[pl.BlockSpec((tmm, d), rows),
                  pl.BlockSpec((None, 1, d_ff2), lambda t, te, nx, vr, nu: (te[t], 0, 0)),
                  pl.BlockSpec((None, 1, d), lambda t, te, nx, vr, nu: (te[t], 0, 0)),
                  any_spec, any_spec],
        out_specs=pl.BlockSpec((tmm, d), lambda t, te, nx, vr, nu: (t, 0)),
        scratch_shapes=[pltpu.VMEM((d, d_ff2), F32), pltpu.VMEM((d_ff, d), F32),
                        pltpu.VMEM((d, d_ff2), BF16), pltpu.VMEM((d_ff, d), BF16),
                        pltpu.SemaphoreType.DMA((2,))])
    return pl.pallas_call(
        _expert_kernel, out_shape=jax.ShapeDtypeStruct((n_slots, d), F32), grid_spec=grid_spec,
        name="moe_experts", compiler_params=_params(1),
    )(tile_expert, next_expert, valid_rows, n_used, xs, b_gu.reshape(n_exp, 1, d_ff2), b_d.reshape(n_exp, 1, d),
      w_gu, w_d)


def _combine_kernel(slot_ref, next_slot_ref, route_ref, hp_ref, ys_ref, gfin_ref, o_ref, buf, sem, *, steps):
    i = pl.program_id(0)
    tm = hp_ref.shape[0]

    def gather(slots, b):
        def body(r, carry):
            for k in range(TOP_K):
                s = slots[r * TOP_K + k]
                pltpu.make_async_copy(ys_ref.at[pl.ds(s, 1)], buf.at[b, k, pl.ds(r, 1)],
                                      sem.at[b]).start(priority=k % 2)
            return carry

        lax.fori_loop(0, tm, body, 0, unroll=4)

    @pl.when(i == 0)
    def _():
        gather(slot_ref, 0)

    if steps > 1:
        @pl.when(i + 1 < steps)
        def _():
            gather(next_slot_ref, (i + 1) % 2)

    b = i % 2
    for k in range(TOP_K):
        pltpu.make_async_copy(ys_ref.at[pl.ds(0, tm)], buf.at[b, k], sem.at[b]).wait()
    route = route_ref[...]
    acc = hp_ref[...]
    for k in range(TOP_K):
        acc = acc + route[:, ROUTE_GATE + k:ROUTE_GATE + k + 1] * buf[b, k]
    o_ref[...] = _rms(acc, gfin_ref[...])


def _combine(slots, route, hp, ys, g_fin):
    n, d = hp.shape
    tm = min(COMBINE_ROWS, n)
    steps = n // tm
    return pl.pallas_call(
        functools.partial(_combine_kernel, steps=steps),
        out_shape=jax.ShapeDtypeStruct((n, d), F32),
        grid=(steps,),
        in_specs=[pl.BlockSpec((tm * TOP_K,), lambda i: (i,), memory_space=pltpu.SMEM),
                  pl.BlockSpec((tm * TOP_K,), lambda i: (jnp.minimum(i + 1, steps - 1),), memory_space=pltpu.SMEM),
                  pl.BlockSpec((tm, LANES), lambda i: (i, 0)),
                  pl.BlockSpec((tm, d), lambda i: (i, 0)),
                  pl.BlockSpec(memory_space=pl.ANY),
                  pl.BlockSpec((1, d), lambda i: (0, 0))],
        out_specs=pl.BlockSpec((tm, d), lambda i: (i, 0)),
        scratch_shapes=[pltpu.VMEM((2, TOP_K, tm, d), F32), pltpu.SemaphoreType.DMA((2,))],
        name="moe_combine", compiler_params=_params(1),
    )(slots, slots, route, hp, ys, g_fin)


def _slot_tables(routes, counts, n_exp):
    tmm = MOE_TILE
    counts = [c.reshape(n_exp).astype(jnp.int32) for c in counts]
    total = sum(counts)
    group = ((total + tmm - 1) // tmm) * tmm
    ends = jnp.cumsum(group)
    start = ends - group
    slots = []
    before = jnp.zeros_like(total)
    experts = jnp.arange(n_exp, dtype=jnp.int32)
    for route, cnt in zip(routes, counts):
        eid = route[:, ROUTE_EXPERT:ROUTE_EXPERT + TOP_K].astype(jnp.int32)
        rank = route[:, ROUTE_RANK:ROUTE_RANK + TOP_K].astype(jnp.int32)
        first = jnp.sum(jnp.where(eid[:, :, None] == experts, start + before, 0), axis=-1)
        slots.append((first + rank).reshape(-1))
        before = before + cnt
    n_tokens = sum(r.shape[0] for r in routes)
    max_tiles = (n_tokens * TOP_K) // tmm + n_exp
    n_used = ends[-1] // tmm
    tile_start = jnp.minimum(jnp.arange(max_tiles, dtype=jnp.int32) * tmm, ends[-1] - 1)
    tile_expert = jnp.sum((ends[None, :] <= tile_start[:, None]).astype(jnp.int32), axis=1)
    group_end = jnp.sum(jnp.where(tile_expert[:, None] == experts, ends, 0), axis=1)
    following = jnp.sum((ends[None, :] <= group_end[:, None]).astype(jnp.int32), axis=1)
    next_expert = jnp.where(group_end < ends[-1], following, -1).astype(jnp.int32)
    filled_end = jnp.sum(jnp.where(tile_expert[:, None] == experts, start + total, 0), axis=1)
    valid_rows = jnp.clip(filled_end - jnp.arange(max_tiles, dtype=jnp.int32) * tmm, 0, tmm).astype(jnp.int32)
    return (slots, tile_expert, next_expert, valid_rows, n_used.reshape(1).astype(jnp.int32), max_tiles * tmm)


def _prep_layer(norm_mix_g, w_in, conv_w, q_norm_g, kv_norm_g, w_uq, w_ukv, out_norm_conv_g, out_norm_attn_g,
                w_out, norm_ffn_g, w_router, b_router):
    d = w_in.shape[0]
    d_conv = conv_w.shape[1]
    q_lora = q_norm_g.shape[0]
    kv_lora = kv_norm_g.shape[0]
    c_kr = 3 * d_conv + q_lora + kv_lora
    w_kr = w_in[:, c_kr:c_kr + QK_ROPE]
    w_kr_rot = jnp.concatenate([-w_kr[:, HALF_ROPE:], w_kr[:, :HALF_ROPE]], axis=1)
    w_in_pad = jnp.concatenate([w_in[:, :c_kr], w_kr, w_kr_rot, jnp.zeros((d, LANES - 2 * QK_ROPE), F32)], axis=1)

    qk = QK_NOPE + QK_ROPE
    wq = w_uq.reshape(q_lora, N_HEADS, qk)
    pad = jnp.zeros((q_lora, N_HEADS, LANES - qk), F32)
    wq_pad = jnp.concatenate([wq, pad], axis=2)
    r1 = wq[:, :, QK_NOPE:QK_NOPE + HALF_ROPE]
    r2 = wq[:, :, QK_NOPE + HALF_ROPE:]
    wq_rot = jnp.concatenate([jnp.zeros((q_lora, N_HEADS, QK_NOPE), F32), -r2, r1, pad], axis=2)
    wq2 = jnp.concatenate([wq_pad.reshape(q_lora, -1), wq_rot.reshape(q_lora, -1)], axis=1)

    wkv = w_ukv.reshape(kv_lora, N_HEADS, QK_NOPE + V_DIM)
    w_uk = wkv[:, :, :QK_NOPE]
    w_uv = wkv[:, :, QK_NOPE:]
    wk_pad = jnp.concatenate([w_uk, jnp.zeros((kv_lora, N_HEADS, LANES - QK_NOPE), F32)], axis=2)
    wv_pad = jnp.concatenate([w_uv, jnp.zeros((kv_lora, N_HEADS, LANES - V_DIM), F32)], axis=2)
    v_one = jnp.zeros((1, N_HEADS, LANES), F32).at[:, :, V_DIM].set(1.0)
    place = jnp.zeros((QK_ROPE, N_HEADS, LANES), F32)
    place = place.at[jnp.arange(QK_ROPE), :, QK_NOPE + jnp.arange(QK_ROPE)].set(1.0)
    wuk_t = jnp.concatenate([jnp.transpose(w_uk, (1, 2, 0)),
                             jnp.zeros((N_HEADS, LANES - QK_NOPE, kv_lora), F32)], axis=1)
    eye = jnp.eye(N_HEADS, dtype=F32)
    w_uv_bd = jnp.einsum("chv,hg->hcgv", w_uv, eye).reshape(N_HEADS * kv_lora, N_HEADS * V_DIM)
    wr_hi = w_router.astype(BF16)
    wr_lo = (w_router - wr_hi.astype(F32)).astype(BF16)
    return {
        "d_conv": d_conv, "q_lora": q_lora, "kv_lora": kv_lora,
        "g_mix": norm_mix_g.reshape(1, -1), "w_in": w_in_pad.astype(BF16), "conv_w": conv_w,
        "g_q": q_norm_g.reshape(1, -1), "g_kv": kv_norm_g.reshape(1, -1), "wq2": wq2.astype(BF16),
        "g_conv": out_norm_conv_g.reshape(1, -1), "wk_pad": wk_pad.reshape(kv_lora, -1).astype(BF16),
        "place": place.reshape(QK_ROPE, -1).astype(BF16), "wv_pad": wv_pad.reshape(kv_lora, -1).astype(BF16),
        "v_one": v_one.reshape(1, -1), "wuk_t": wuk_t.astype(BF16),
        "g_attn": out_norm_attn_g.reshape(1, -1), "wo_conv": w_out[:d_conv, :].astype(BF16),
        "wo_attn": w_out[d_conv:, :].astype(BF16), "w_uv_bd": w_uv_bd.astype(BF16),
        "g_ffn": norm_ffn_g.reshape(1, -1), "wr_hi": wr_hi, "wr_lo": wr_lo, "b_router": b_router.reshape(1, -1),
    }


def kernel(x_prompt, x_sample, cache_kv_latent, cache_k_rope, state_conv, page_table, norm_mix_g, w_in, conv_w,
           q_norm_g, kv_norm_g, w_uq, w_ukv, out_norm_conv_g, out_norm_attn_g, w_out, norm_ffn_g, w_router,
           b_router, w_gate_up, b_gate_up, w_down, b_down, final_norm_g):
    depth = w_in.shape[0]
    batch, seq, d = x_prompt.shape
    dec_b, dec_t, _ = x_sample.shape
    n_exp = w_router.shape[-1]
    past_len = page_table.shape[1] * cache_kv_latent.shape[2]
    assert batch == 1, "the prompt path lays one sequence out along rows"
    assert depth == 1, "deeper stacks need the un-normalised residual between layers"
    pos_p = jnp.arange(seq, dtype=jnp.int32)
    pos_s = jnp.tile(past_len + jnp.arange(dec_t, dtype=jnp.int32), dec_b)
    hp = x_prompt.reshape(seq, d)
    hs = x_sample.reshape(dec_b * dec_t, d)
    g_fin = final_norm_g.reshape(1, -1)
    outs = [[] for _ in range(6)]
    for l in range(depth):
        w = _prep_layer(norm_mix_g[l], w_in[l], conv_w[l], q_norm_g[l], kv_norm_g[l], w_uq[l], w_ukv[l],
                        out_norm_conv_g[l], out_norm_attn_g[l], w_out[l], norm_ffn_g[l], w_router[l], b_router[l])
        d_conv, kv_lora = w["d_conv"], w["kv_lora"]

        u_p, ycn_p, q_p, ckv_p, kr_p, k_p, v_p = _front(hp, pos_p, w, prompt=True)
        ya_p = _flash(q_p, k_p, v_p)
        hp_mid, xn_p, route_p, count_p = _back(hp, ycn_p, ya_p, w, latent_values=False)

        st = state_conv[l]
        t_idx = jnp.arange(dec_t)
        ov1 = jnp.where((t_idx == 0)[None, :, None], st[:, 1:2, :], 0.0).reshape(dec_b * dec_t, d_conv)
        ov2 = jnp.where((t_idx == 0)[None, :, None], st[:, 0:1, :],
                        jnp.where((t_idx == 1)[None, :, None], st[:, 1:2, :], 0.0)).reshape(dec_b * dec_t, d_conv)
        u_s, ycn_s, q_s, ckv_s, kr_s, qlat_s = _front(hs, pos_s, w, prompt=False, ov=(ov1, ov2), seq_rows=dec_t)
        rows = dec_t * N_HEADS
        qr_s = q_s.reshape(dec_b, rows, LANES)[:, :, QK_NOPE:QK_NOPE + QK_ROPE]
        cache_krt = jnp.swapaxes(cache_k_rope[l], 1, 2)
        o_lat = _paged_attention(qlat_s.reshape(dec_b, rows, kv_lora), qr_s,
                                 ckv_s.reshape(dec_b, dec_t, kv_lora), kr_s.reshape(dec_b, dec_t, QK_ROPE),
                                 cache_kv_latent[l], cache_krt, page_table)
        ya_s = o_lat.reshape(dec_b * dec_t, N_HEADS * kv_lora)
        hs_mid, xn_s, route_s, count_s = _back(hs, ycn_s, ya_s, w, latent_values=True)

        (slot_p, slot_s), tile_expert, next_expert, valid_rows, n_used, n_slots = _slot_tables(
            (route_p, route_s), (count_p, count_s), n_exp)
        xs = jnp.zeros((n_slots, d), F32)
        xs = _dispatch(xn_p, slot_p, xs)
        xs = _dispatch(xn_s, slot_s, xs)
        ys = _experts(xs, tile_expert, next_expert, valid_rows, n_used,
                      w_gate_up[l], b_gate_up[l], w_down[l], b_down[l])
        hp = _combine(slot_p, route_p, hp_mid, ys, g_fin)
        hs = _combine(slot_s, route_s, hs_mid, ys, g_fin)

        outs[0].append(ckv_p.reshape(batch, seq, kv_lora))
        outs[1].append(kr_p.reshape(batch, seq, QK_ROPE))
        outs[2].append(u_p[seq - 2:, :].reshape(batch, 2, d_conv))
        outs[3].append(ckv_s.reshape(dec_b, dec_t, kv_lora))
        outs[4].append(kr_s.reshape(dec_b, dec_t, QK_ROPE))
        outs[5].append(u_s.reshape(dec_b, dec_t, d_conv)[:, dec_t - 2:, :])
    return (hp.reshape(batch, seq, d), hs.reshape(dec_b, dec_t, d)) + tuple(jnp.stack(o) for o in outs)
```

```python
import functools
import math

import jax
import jax.numpy as jnp
from jax import lax
from jax.experimental import pallas as pl
from jax.experimental.pallas import tpu as pltpu

N_HEADS = 8
QK_NOPE = 64
QK_ROPE = 32
V_DIM = 64
HALF_ROPE = QK_ROPE // 2
ROPE_THETA = 10000.0
SOFTMAX_SCALE = (QK_NOPE + QK_ROPE) ** -0.5
LOG2E = math.log2(math.e)
TOP_K = 4
SWIGLU_LIMIT = 7.0
SWIGLU_ALPHA = 1.702
RMS_EPS = 1e-6
NEG = -1e30

LANES = 128
MIB = 1024 * 1024
VMEM_LIMIT = 56 * MIB
F32 = jnp.float32
BF16 = jnp.bfloat16

FRONT_ROWS = 512
FLASH_ROWS = 1024
BACK_ROWS = 1024
BACK_SUB_ROWS = 512
PAGES_PER_CHUNK = 64
MOE_TILE = 512
DISPATCH_ROWS = 512
COMBINE_ROWS = 512
ROUTE_EXPERT, ROUTE_RANK, ROUTE_GATE = 0, TOP_K, 2 * TOP_K

NT_DIMS = (((1,), (1,)), ((), ()))


def _rms(x, g):
    return x * lax.rsqrt(jnp.mean(x * x, axis=-1, keepdims=True) + RMS_EPS) * g


def _dot(a, b):
    return jnp.dot(a, b, preferred_element_type=F32)


def _dot_nt(a, b):
    return lax.dot_general(a, b, NT_DIMS, preferred_element_type=F32)


def _params(n_axes):
    return pltpu.CompilerParams(dimension_semantics=("arbitrary",) * n_axes, vmem_limit_bytes=VMEM_LIMIT)


def _front_kernel(*refs, d_conv, q_lora, kv_lora, seq_rows, prompt):
    if prompt:
        (x_ref, cs_ref, gmix_ref, win_ref, cw_ref, gq_ref, gkv_ref, wq_ref, gconv_ref,
         wk_ref, place_ref, wv_ref, vone_ref,
         u_ref, ycn_ref, q_ref, ckv_ref, kr_ref, k_ref, v_ref, carry_ref) = refs
    else:
        (x_ref, cs_ref, ov1_ref, ov2_ref, gmix_ref, win_ref, cw_ref, gq_ref, gkv_ref, wq_ref,
         gconv_ref, wukt_ref,
         u_ref, ycn_ref, q_ref, ckv_ref, kr_ref, qlat_ref) = refs
    tm = x_ref.shape[0]
    n = _rms(x_ref[...], gmix_ref[...]).astype(BF16)
    z = _dot(n, win_ref[...])
    c0 = 3 * d_conv
    b_g = z[:, 0:d_conv]
    u = z[:, d_conv:2 * d_conv] * z[:, 2 * d_conv:c0]
    z_q = z[:, c0:c0 + q_lora]
    z_kv = z[:, c0 + q_lora:c0 + q_lora + kv_lora]
    z_r = z[:, c0 + q_lora + kv_lora:]
    u_ref[...] = u

    row = lax.broadcasted_iota(jnp.int32, u.shape, 0)
    r1 = pltpu.roll(u, 1, 0)
    r2 = pltpu.roll(u, 2, 0)
    if prompt:
        @pl.when(pl.program_id(0) == 0)
        def _():
            carry_ref[...] = jnp.zeros_like(carry_ref)
        last1 = carry_ref[7:8, :]
        last2 = carry_ref[6:7, :]
        p1 = jnp.where(row == 0, last1, r1)
        p2 = jnp.where(row == 0, last2, jnp.where(row == 1, last1, r2))
        carry_ref[...] = u[tm - 8:tm, :]
    else:
        t = row % seq_rows
        p1 = jnp.where(t == 0, ov1_ref[...], r1)
        p2 = jnp.where(t < 2, ov2_ref[...], r2)
    cw = cw_ref[...]
    y_c = p2 * cw[0:1, :] + p1 * cw[1:2, :] + u * cw[2:3, :]
    ycn_ref[...] = _rms(b_g * y_c, gconv_ref[...]).astype(BF16)

    nq = _rms(z_q, gq_ref[...]).astype(BF16)
    q2 = _dot(nq, wq_ref[...])
    hw = N_HEADS * LANES
    cs = cs_ref[...]
    cos, sin = cs[:, :HALF_ROPE], cs[:, HALF_ROPE:]
    qs = SOFTMAX_SCALE * LOG2E
    pad = jnp.zeros((tm, LANES - QK_NOPE - QK_ROPE), F32)
    tqc = jnp.concatenate([jnp.full((tm, QK_NOPE), qs, F32), cos * qs, cos * qs, pad], axis=1)
    tqs = jnp.concatenate([jnp.zeros((tm, QK_NOPE), F32), sin * qs, sin * qs, pad], axis=1)
    q_heads = []
    for h in range(N_HEADS):
        qa = q2[:, h * LANES:(h + 1) * LANES]
        qb = q2[:, hw + h * LANES:hw + (h + 1) * LANES]
        q_heads.append((qa * tqc + qb * tqs).astype(BF16))
    q_ref[...] = jnp.concatenate(q_heads, axis=1)

    ckv = _rms(z_kv, gkv_ref[...])
    ckv_ref[...] = ckv
    tk = jnp.concatenate([cos, cos, sin, sin, jnp.zeros((tm, LANES - 2 * QK_ROPE), F32)], axis=1)
    zr = z_r * tk
    kr = zr[:, 0:QK_ROPE] + zr[:, QK_ROPE:2 * QK_ROPE]
    kr_ref[...] = kr

    if prompt:
        ckv_b = ckv.astype(BF16)
        k_ref[...] = (_dot(ckv_b, wk_ref[...]) + _dot(kr.astype(BF16), place_ref[...])).astype(BF16)
        v_ref[...] = (_dot(ckv_b, wv_ref[...]) + vone_ref[...]).astype(BF16)
    else:
        for h in range(N_HEADS):
            qlat_ref[:, h * kv_lora:(h + 1) * kv_lora] = _dot(q_heads[h], wukt_ref[h]).astype(BF16)


def _front(x, pos, w, *, prompt, ov=None, seq_rows=0):
    n, d = x.shape
    d_conv, q_lora, kv_lora = w["d_conv"], w["q_lora"], w["kv_lora"]
    tm = min(FRONT_ROWS, n)
    hw = N_HEADS * LANES
    inv_freq = ROPE_THETA ** (-jnp.arange(HALF_ROPE, dtype=F32) / HALF_ROPE)
    ang = pos.astype(F32)[:, None] * inv_freq[None, :]
    cs = jnp.concatenate([jnp.cos(ang), jnp.sin(ang)], axis=1)

    row = lambda c: pl.BlockSpec((tm, c), lambda i: (i, 0))
    full = lambda a: pl.BlockSpec(a.shape, lambda i: (0,) * a.ndim)
    ins = [x, cs]
    in_specs = [row(d), row(2 * HALF_ROPE)]
    if not prompt:
        ins += [ov[0], ov[1]]
        in_specs += [row(d_conv), row(d_conv)]
    shared = [w["g_mix"], w["w_in"], w["conv_w"], w["g_q"], w["g_kv"], w["wq2"], w["g_conv"]]
    shared += [w["wk_pad"], w["place"], w["wv_pad"], w["v_one"]] if prompt else [w["wuk_t"]]
    ins += shared
    in_specs += [full(a) for a in shared]

    outs = [jax.ShapeDtypeStruct((n, d_conv), F32), jax.ShapeDtypeStruct((n, d_conv), BF16),
            jax.ShapeDtypeStruct((n, hw), BF16), jax.ShapeDtypeStruct((n, kv_lora), F32),
            jax.ShapeDtypeStruct((n, QK_ROPE), F32)]
    out_specs = [row(d_conv), row(d_conv), row(hw), row(kv_lora), row(QK_ROPE)]
    scratch = []
    if prompt:
        outs += [jax.ShapeDtypeStruct((n, hw), BF16), jax.ShapeDtypeStruct((n, hw), BF16)]
        out_specs += [row(hw), row(hw)]
        scratch = [pltpu.VMEM((8, d_conv), F32)]
    else:
        outs += [jax.ShapeDtypeStruct((n, N_HEADS * kv_lora), BF16)]
        out_specs += [row(N_HEADS * kv_lora)]
    kern = functools.partial(_front_kernel, d_conv=d_conv, q_lora=q_lora, kv_lora=kv_lora, seq_rows=seq_rows,
                             prompt=prompt)
    return pl.pallas_call(
        kern, out_shape=outs, grid=(n // tm,), in_specs=in_specs, out_specs=out_specs, scratch_shapes=scratch,
        name="front_prompt" if prompt else "front_sample", compiler_params=_params(1),
    )(*ins)


def _flash_kernel(q_ref, k_ref, v_ref, o_ref, m_sc, acc_sc, *, tq):
    i = pl.program_id(1)
    m_sc[...] = jnp.full_like(m_sc, NEG)
    acc_sc[...] = jnp.zeros_like(acc_sc)
    half = tq // 2

    def block(row0, n_rows, key0, n_keys, diagonal):
        rows = slice(row0, row0 + n_rows)
        for h in range(2):
            sl = slice(h * LANES, (h + 1) * LANES)
            s = _dot_nt(q_ref[rows, sl], k_ref[pl.ds(key0, n_keys), sl])
            if diagonal:
                r = lax.broadcasted_iota(jnp.int32, s.shape, 0)
                c = lax.broadcasted_iota(jnp.int32, s.shape, 1)
                s = jnp.where(c <= r, s, NEG)
            m_prev = m_sc[h, rows]
            m_new = jnp.maximum(m_prev, jnp.max(s, axis=-1, keepdims=True))
            alpha = jnp.exp2(m_prev - m_new)
            p = jnp.exp2(s - jnp.concatenate([m_new] * (n_keys // LANES), axis=1))
            acc_sc[h, rows] = alpha * acc_sc[h, rows] + _dot(p.astype(BF16), v_ref[pl.ds(key0, n_keys), sl])
            m_sc[h, rows] = m_new

    def full_block(j):
        block(0, tq, pl.multiple_of(j * tq, tq), tq, False)

    def body(jj, carry):
        full_block(2 * jj)
        full_block(2 * jj + 1)
        return carry

    lax.fori_loop(0, i // 2, body, 0)

    @pl.when(i % 2 == 1)
    def _():
        full_block(i - 1)

    diag = pl.multiple_of(i * tq, tq)
    block(0, tq, diag, half, True)
    block(half, half, diag + half, half, True)
    outs = []
    for h in range(2):
        a = acc_sc[h]
        outs.append(a[:, :V_DIM] / a[:, V_DIM:V_DIM + 1])
    o_ref[...] = jnp.concatenate(outs, axis=1)


def _flash(q, k, v):
    s = q.shape[0]
    tq = min(FLASH_ROWS, s)
    pairs = N_HEADS // 2
    return pl.pallas_call(
        functools.partial(_flash_kernel, tq=tq),
        out_shape=jax.ShapeDtypeStruct((s, N_HEADS * V_DIM), F32),
        grid=(pairs, s // tq),
        in_specs=[pl.BlockSpec((tq, 2 * LANES), lambda p, i: (i, p)),
                  pl.BlockSpec((s, 2 * LANES), lambda p, i: (0, p)),
                  pl.BlockSpec((s, 2 * LANES), lambda p, i: (0, p))],
        out_specs=pl.BlockSpec((tq, 2 * V_DIM), lambda p, i: (i, p)),
        scratch_shapes=[pltpu.VMEM((2, tq, LANES), F32), pltpu.VMEM((2, tq, LANES), F32)],
        name="flash_prompt", compiler_params=_params(2),
    )(q, k, v)


def _paged_kernel(pt_ref, qlat_ref, qr_ref, cnew_ref, krnew_ref, cache_c, cache_krt, o_ref, cbuf, krbuf, cbf, sem,
                  *, nb, n_pages, page, ppc, t_new):
    b = pl.program_id(0)
    slot = b % 2
    n_chunks = n_pages // ppc
    ck = ppc * page

    def fetch(bb, sl):
        for pg_i in range(n_pages):
            pg = pt_ref[bb, pg_i]
            ci, p = divmod(pg_i, ppc)
            pltpu.make_async_copy(cache_c.at[pg], cbuf.at[sl, pg_i], sem.at[sl]).start()
            pltpu.make_async_copy(cache_krt.at[pg], krbuf.at[sl, ci, :, pl.ds(p * page, page)], sem.at[sl]).start()

    @pl.when(b == 0)
    def _():
        fetch(0, 0)

    if nb > 1:
        @pl.when(b + 1 < nb)
        def _():
            fetch(b + 1, 1 - slot)

    pltpu.make_async_copy(cache_c.at[pl.ds(0, n_pages)], cbuf.at[slot], sem.at[slot]).wait()
    pltpu.make_async_copy(krbuf.at[1 - slot], krbuf.at[slot], sem.at[slot]).wait()

    qlat = qlat_ref[...]
    qr = qr_ref[...]
    rows, kv_lora = qlat.shape

    def scores(ci):
        cb = cbuf[slot, pl.ds(ci * ppc, ppc)].reshape(ck, kv_lora).astype(BF16)
        cbf[ci % 2] = cb
        return _dot_nt(qlat, cb) + _dot(qr, krbuf[slot, ci].astype(BF16))

    def absorb(ci, s, m_prev, l_prev, acc):
        m_new = jnp.maximum(m_prev, jnp.max(s, axis=-1, keepdims=True))
        alpha = jnp.exp2(m_prev - m_new)
        p_ = jnp.exp2(s - m_new)
        l_new = alpha * l_prev + jnp.sum(p_, axis=-1, keepdims=True)
        return m_new, l_new, alpha * acc + _dot(p_.astype(BF16), cbf[ci % 2])

    def chunk(ci, carry):
        s, m_prev, l_prev, acc = carry
        s_next = scores(ci + 1)
        return (s_next,) + absorb(ci, s, m_prev, l_prev, acc)

    init = (scores(0), jnp.full((rows, 1), NEG, F32), jnp.zeros((rows, 1), F32), jnp.zeros((rows, kv_lora), F32))
    s_last, m_old, l_old, acc = lax.fori_loop(0, n_chunks - 1, chunk, init)
    m_old, l_old, acc = absorb(n_chunks - 1, s_last, m_old, l_old, acc)

    ql = qlat.astype(F32)
    qrf = qr.astype(F32)
    cn = cnew_ref[...]
    kn = krnew_ref[...]
    t_row = lax.broadcasted_iota(jnp.int32, (rows, 1), 0) // N_HEADS
    s_new = []
    for j in range(t_new):
        sj = (jnp.sum(ql * cn[j:j + 1, :], axis=-1, keepdims=True)
              + jnp.sum(qrf * kn[j:j + 1, :], axis=-1, keepdims=True))
        s_new.append(jnp.where(t_row >= j, sj, NEG))
    m_fin = m_old
    for sj in s_new:
        m_fin = jnp.maximum(m_fin, sj)
    a = jnp.exp2(m_old - m_fin)
    l_fin = a * l_old
    acc = a * acc
    for j, sj in enumerate(s_new):
        pj = jnp.exp2(sj - m_fin)
        l_fin = l_fin + pj
        acc = acc + pj * cn[j:j + 1, :]
    o_ref[...] = acc / l_fin


def _paged_attention(qlat, qr, c_new, kr_new, cache_c, cache_krt, page_table):
    nb, rows, kv_lora = qlat.shape
    n_pages = page_table.shape[1]
    page = cache_c.shape[1]
    t_new = c_new.shape[1]
    ppc = math.gcd(PAGES_PER_CHUNK, n_pages)
    per_b = lambda r, w: pl.BlockSpec((None, r, w), lambda b, pt: (b, 0, 0))
    any_spec = pl.BlockSpec(memory_space=pl.ANY)
    grid_spec = pltpu.PrefetchScalarGridSpec(
        num_scalar_prefetch=1, grid=(nb,),
        in_specs=[per_b(rows, kv_lora), per_b(rows, QK_ROPE), per_b(t_new, kv_lora), per_b(t_new, QK_ROPE),
                  any_spec, any_spec],
        out_specs=per_b(rows, kv_lora),
        scratch_shapes=[pltpu.VMEM((2, n_pages, page, kv_lora), F32),
                        pltpu.VMEM((2, n_pages // ppc, QK_ROPE, ppc * page), F32),
                        pltpu.VMEM((2, ppc * page, kv_lora), BF16),
                        pltpu.SemaphoreType.DMA((2,))])
    return pl.pallas_call(
        functools.partial(_paged_kernel, nb=nb, n_pages=n_pages, page=page, ppc=ppc, t_new=t_new),
        out_shape=jax.ShapeDtypeStruct((nb, rows, kv_lora), F32), grid_spec=grid_spec, name="paged_sample",
        compiler_params=_params(1),
    )(page_table, qlat, qr, c_new, kr_new, cache_c, cache_krt)


def _back_kernel(*refs, latent_values, sub):
    if latent_values:
        (h_ref, ycn_ref, ya_ref, wuv_ref, gattn_ref, woc_ref, woa_ref, gffn_ref, wrh_ref, wrl_ref, br_ref, tri_ref,
         hp_ref, xn_ref, route_ref, count_ref, base_sc) = refs
    else:
        (h_ref, ycn_ref, ya_ref, gattn_ref, woc_ref, woa_ref, gffn_ref, wrh_ref, wrl_ref, br_ref, tri_ref,
         hp_ref, xn_ref, route_ref, count_ref, base_sc) = refs

    @pl.when(pl.program_id(0) == 0)
    def _():
        base_sc[...] = jnp.zeros_like(base_sc)

    def sub_tile(rows, base):
        if latent_values:
            ya = _dot(ya_ref[rows, :].astype(BF16), wuv_ref[...])
        else:
            ya = ya_ref[rows, :]
        yan = _rms(ya, gattn_ref[...]).astype(BF16)
        hp = h_ref[rows, :] + _dot(ycn_ref[rows, :], woc_ref[...]) + _dot(yan, woa_ref[...])
        hp_ref[rows, :] = hp
        xn = _rms(hp, gffn_ref[...])
        xn_ref[rows, :] = xn
        x_hi = xn.astype(BF16)
        x_lo = (xn - x_hi.astype(F32)).astype(BF16)
        logits = _dot(x_hi, wrh_ref[...]) + _dot(x_hi, wrl_ref[...]) + _dot(x_lo, wrh_ref[...]) + br_ref[...]
        n_exp = logits.shape[1]
        lane = lax.broadcasted_iota(jnp.int32, logits.shape, 1).astype(F32)
        work = logits
        picks = []
        for _ in range(TOP_K):
            vmax = jnp.max(work, axis=-1, keepdims=True)
            idx = jnp.min(jnp.where(work == vmax, lane, float(n_exp)), axis=-1, keepdims=True)
            sel = lane == idx
            picks.append((vmax, idx, sel))
            work = jnp.where(sel, -jnp.inf, work)
        top = picks[0][0]
        exps = [jnp.exp(v - top) for v, _, _ in picks]
        denom = exps[0]
        for e in exps[1:]:
            denom = denom + e

        onehot = jnp.zeros_like(logits)
        for _, _, sel in picks:
            onehot = jnp.where(sel, 1.0, onehot)
        rank_all = _dot(tri_ref[...], onehot.astype(BF16)) + base

        lane_r = lax.broadcasted_iota(jnp.int32, (sub, LANES), 1)
        route = jnp.zeros((sub, LANES), F32)
        for k, (e, (_, idx, sel)) in enumerate(zip(exps, picks)):
            rank_k = jnp.sum(jnp.where(sel, rank_all, 0.0), axis=-1, keepdims=True)
            route = jnp.where(lane_r == ROUTE_EXPERT + k, idx, route)
            route = jnp.where(lane_r == ROUTE_RANK + k, rank_k, route)
            route = jnp.where(lane_r == ROUTE_GATE + k, e / denom, route)
        route_ref[rows, :] = route
        return base + jnp.sum(onehot, axis=0, keepdims=True)

    base = base_sc[...]
    for s in range(h_ref.shape[0] // sub):
        base = sub_tile(slice(s * sub, (s + 1) * sub), base)
    base_sc[...] = base
    count_ref[...] = base


def _back(h, ycn, ya, w, *, latent_values):
    n, d = h.shape
    tm = min(BACK_ROWS, n)
    sub = min(BACK_SUB_ROWS, tm)
    n_exp = w["wr_hi"].shape[1]
    row = lambda c: pl.BlockSpec((tm, c), lambda i: (i, 0))
    full = lambda a: pl.BlockSpec(a.shape, lambda i: (0,) * a.ndim)
    tri = jnp.tril(jnp.ones((sub, sub), F32), k=-1).astype(BF16)
    shared = [w["g_attn"], w["wo_conv"], w["wo_attn"], w["g_ffn"], w["wr_hi"], w["wr_lo"], w["b_router"], tri]
    if latent_values:
        shared = [w["w_uv_bd"]] + shared
    return pl.pallas_call(
        functools.partial(_back_kernel, latent_values=latent_values, sub=sub),
        out_shape=[jax.ShapeDtypeStruct((n, d), F32), jax.ShapeDtypeStruct((n, d), F32),
                   jax.ShapeDtypeStruct((n, LANES), F32), jax.ShapeDtypeStruct((1, n_exp), F32)],
        grid=(n // tm,),
        in_specs=[row(d), row(ycn.shape[1]), row(ya.shape[1])] + [full(a) for a in shared],
        out_specs=[row(d), row(d), row(LANES), pl.BlockSpec((1, n_exp), lambda i: (0, 0))],
        scratch_shapes=[pltpu.VMEM((1, n_exp), F32)],
        name="back_sample" if latent_values else "back_prompt", compiler_params=_params(1),
    )(h, ycn, ya, *shared)


def _dispatch_kernel(slot_ref, x_ref, xs_in_ref, xs_ref, sem):
    del xs_in_ref
    tm = x_ref.shape[0]

    def body(r, carry):
        for k in range(TOP_K):
            s = slot_ref[r * TOP_K + k]
            pltpu.make_async_copy(x_ref.at[pl.ds(r, 1)], xs_ref.at[pl.ds(s, 1)], sem).start(priority=k % 2)
        return carry

    lax.fori_loop(0, tm, body, 0, unroll=4)
    for k in range(TOP_K):
        pltpu.make_async_copy(x_ref, xs_ref.at[pl.ds(0, tm)], sem).wait()


def _dispatch(x, slots, xs):
    n, d = x.shape
    tm = min(DISPATCH_ROWS, n)
    return pl.pallas_call(
        _dispatch_kernel,
        out_shape=jax.ShapeDtypeStruct(xs.shape, F32),
        grid=(n // tm,),
        in_specs=[pl.BlockSpec((tm * TOP_K,), lambda i: (i,), memory_space=pltpu.SMEM),
                  pl.BlockSpec((tm, d), lambda i: (i, 0)),
                  pl.BlockSpec(memory_space=pl.ANY)],
        out_specs=pl.BlockSpec(memory_space=pl.ANY),
        scratch_shapes=[pltpu.SemaphoreType.DMA(())],
        input_output_aliases={2: 0},
        name="moe_dispatch", compiler_params=_params(1),
    )(slots, x, xs)


def _expert_kernel(te_ref, nx_ref, vr_ref, nu_ref, xs_ref, bgu_ref, bd_ref, wgu_hbm, wd_hbm, ys_ref,
                   wgu_st, wd_st, wgu_bf, wd_bf, sem):
    ti = pl.program_id(0)
    used = ti < nu_ref[0]
    new_expert = (ti == 0) | (te_ref[ti] != te_ref[jnp.maximum(ti - 1, 0)])
    d_ff = wd_bf.shape[0]

    def weight_copies(e):
        return (pltpu.make_async_copy(wgu_hbm.at[e], wgu_st, sem.at[0]),
                pltpu.make_async_copy(wd_hbm.at[e], wd_st, sem.at[1]))

    @pl.when(ti == 0)
    def _():
        for cp in weight_copies(te_ref[0]):
            cp.start()

    @pl.when(used & new_expert)
    def _():
        for cp in weight_copies(te_ref[ti]):
            cp.wait()
        wgu_bf[...] = wgu_st[...].astype(BF16)
        wd_bf[...] = wd_st[...].astype(BF16)

        @pl.when(nx_ref[ti] >= 0)
        def _():
            for cp in weight_copies(nx_ref[ti]):
                cp.start()

    def mlp(x):
        x = x.astype(BF16)
        bgu = bgu_ref[...]
        gate = jnp.minimum(_dot(x, wgu_bf[:, :d_ff]) + bgu[:, :d_ff], SWIGLU_LIMIT)
        up = jnp.clip(_dot(x, wgu_bf[:, d_ff:]) + bgu[:, d_ff:], -SWIGLU_LIMIT, SWIGLU_LIMIT)
        hid = (up + 1.0) * gate * jax.nn.sigmoid(SWIGLU_ALPHA * gate)
        return _dot(hid.astype(BF16), wd_bf[...]) + bd_ref[...]

    half = xs_ref.shape[0] // 2
    valid = vr_ref[ti]

    @pl.when(used & (valid > half))
    def _():
        ys_ref[...] = mlp(xs_ref[...])

    @pl.when(used & (valid <= half))
    def _():
        ys_ref[:half, :] = mlp(xs_ref[:half, :])
        ys_ref[half:, :] = jnp.zeros((half, ys_ref.shape[1]), F32)

    @pl.when(jnp.logical_not(used))
    def _():
        ys_ref[...] = jnp.zeros_like(ys_ref)


def _experts(xs, tile_expert, next_expert, valid_rows, n_used, w_gu, b_gu, w_d, b_d):
    n_slots, d = xs.shape
    n_exp, _, d_ff2 = w_gu.shape
    d_ff = d_ff2 // 2
    tmm = MOE_TILE
    rows = lambda t, te, nx, vr, nu: (jnp.minimum(t, nu[0] - 1), 0)
    any_spec = pl.BlockSpec(memory_space=pl.ANY)
    grid_spec = pltpu.PrefetchScalarGridSpec(
        num_scalar_prefetch=4, grid=(n_slots // tmm,),
        in_specs=[pl.BlockSpec((tmm, d), rows),
                  pl.BlockSpec((None, 1, d_ff2), lambda t, te, nx, vr, nu: (te[t], 0, 0)),
                  pl.BlockSpec((None, 1, d), lambda t, te, nx, vr, nu: (te[t], 0, 0)),
                  any_spec, any_spec],
        out_specs=pl.BlockSpec((tmm, d), lambda t, te, nx, vr, nu: (t, 0)),
        scratch_shapes=[pltpu.VMEM((d, d_ff2), F32), pltpu.VMEM((d_ff, d), F32),
                        pltpu.VMEM((d, d_ff2), BF16), pltpu.VMEM((d_ff, d), BF16),
                        pltpu.SemaphoreType.DMA((2,))])
    return pl.pallas_call(
        _expert_kernel, out_shape=jax.ShapeDtypeStruct((n_slots, d), F32), grid_spec=grid_spec,
        name="moe_experts", compiler_params=_params(1),
    )(tile_expert, next_expert, valid_rows, n_used, xs, b_gu.reshape(n_exp, 1, d_ff2), b_d.reshape(n_exp, 1, d),
      w_gu, w_d)


def _combine_kernel(slot_ref, next_slot_ref, route_ref, hp_ref, ys_ref, gfin_ref, o_ref, buf, sem, *, steps):
    i = pl.program_id(0)
    tm = hp_ref.shape[0]

    def gather(slots, b):
        def body(r, carry):
            for k in range(TOP_K):
                s = slots[r * TOP_K + k]
                pltpu.make_async_copy(ys_ref.at[pl.ds(s, 1)], buf.at[b, k, pl.ds(r, 1)],
                                      sem.at[b]).start(priority=k % 2)
            return carry

        lax.fori_loop(0, tm, body, 0, unroll=4)

    @pl.when(i == 0)
    def _():
        gather(slot_ref, 0)

    if steps > 1:
        @pl.when(i + 1 < steps)
        def _():
            gather(next_slot_ref, (i + 1) % 2)

    b = i % 2
    for k in range(TOP_K):
        pltpu.make_async_copy(ys_ref.at[pl.ds(0, tm)], buf.at[b, k], sem.at[b]).wait()
    route = route_ref[...]
    acc = hp_ref[...]
    for k in range(TOP_K):
        acc = acc + route[:, ROUTE_GATE + k:ROUTE_GATE + k + 1] * buf[b, k]
    o_ref[...] = _rms(acc, gfin_ref[...])


def _combine(slots, route, hp, ys, g_fin):
    n, d = hp.shape
    tm = min(COMBINE_ROWS, n)
    steps = n // tm
    return pl.pallas_call(
        functools.partial(_combine_kernel, steps=steps),
        out_shape=jax.ShapeDtypeStruct((n, d), F32),
        grid=(steps,),
        in_specs=[pl.BlockSpec((tm * TOP_K,), lambda i: (i,), memory_space=pltpu.SMEM),
                  pl.BlockSpec((tm * TOP_K,), lambda i: (jnp.minimum(i + 1, steps - 1),), memory_space=pltpu.SMEM),
                  pl.BlockSpec((tm, LANES), lambda i: (i, 0)),
                  pl.BlockSpec((tm, d), lambda i: (i, 0)),
                  pl.BlockSpec(memory_space=pl.ANY),
                  pl.BlockSpec((1, d), lambda i: (0, 0))],
        out_specs=pl.BlockSpec((tm, d), lambda i: (i, 0)),
        scratch_shapes=[pltpu.VMEM((2, TOP_K, tm, d), F32), pltpu.SemaphoreType.DMA((2,))],
        name="moe_combine", compiler_params=_params(1),
    )(slots, slots, route, hp, ys, g_fin)


def _slot_tables(routes, counts, n_exp):
    tmm = MOE_TILE
    counts = [c.reshape(n_exp).astype(jnp.int32) for c in counts]
    total = sum(counts)
    group = ((total + tmm - 1) // tmm) * tmm
    ends = jnp.cumsum(group)
    start = ends - group
    slots = []
    before = jnp.zeros_like(total)
    experts = jnp.arange(n_exp, dtype=jnp.int32)
    for route, cnt in zip(routes, counts):
        eid = route[:, ROUTE_EXPERT:ROUTE_EXPERT + TOP_K].astype(jnp.int32)
        rank = route[:, ROUTE_RANK:ROUTE_RANK + TOP_K].astype(jnp.int32)
        first = jnp.sum(jnp.where(eid[:, :, None] == experts, start + before, 0), axis=-1)
        slots.append((first + rank).reshape(-1))
        before = before + cnt
    n_tokens = sum(r.shape[0] for r in routes)
    max_tiles = (n_tokens * TOP_K) // tmm + n_exp
    n_used = ends[-1] // tmm
    tile_start = jnp.minimum(jnp.arange(max_tiles, dtype=jnp.int32) * tmm, ends[-1] - 1)
    tile_expert = jnp.sum((ends[None, :] <= tile_start[:, None]).astype(jnp.int32), axis=1)
    group_end = jnp.sum(jnp.where(tile_expert[:, None] == experts, ends, 0), axis=1)
    following = jnp.sum((ends[None, :] <= group_end[:, None]).astype(jnp.int32), axis=1)
    next_expert = jnp.where(group_end < ends[-1], following, -1).astype(jnp.int32)
    filled_end = jnp.sum(jnp.where(tile_expert[:, None] == experts, start + total, 0), axis=1)
    valid_rows = jnp.clip(filled_end - jnp.arange(max_tiles, dtype=jnp.int32) * tmm, 0, tmm).astype(jnp.int32)
    return (slots, tile_expert, next_expert, valid_rows, n_used.reshape(1).astype(jnp.int32), max_tiles * tmm)


def _prep_layer(norm_mix_g, w_in, conv_w, q_norm_g, kv_norm_g, w_uq, w_ukv, out_norm_conv_g, out_norm_attn_g,
                w_out, norm_ffn_g, w_router, b_router):
    d = w_in.shape[0]
    d_conv = conv_w.shape[1]
    q_lora = q_norm_g.shape[0]
    kv_lora = kv_norm_g.shape[0]
    c_kr = 3 * d_conv + q_lora + kv_lora
    w_kr = w_in[:, c_kr:c_kr + QK_ROPE]
    w_kr_rot = jnp.concatenate([-w_kr[:, HALF_ROPE:], w_kr[:, :HALF_ROPE]], axis=1)
    w_in_pad = jnp.concatenate([w_in[:, :c_kr], w_kr, w_kr_rot, jnp.zeros((d, LANES - 2 * QK_ROPE), F32)], axis=1)

    qk = QK_NOPE + QK_ROPE
    wq = w_uq.reshape(q_lora, N_HEADS, qk)
    pad = jnp.zeros((q_lora, N_HEADS, LANES - qk), F32)
    wq_pad = jnp.concatenate([wq, pad], axis=2)
    r1 = wq[:, :, QK_NOPE:QK_NOPE + HALF_ROPE]
    r2 = wq[:, :, QK_NOPE + HALF_ROPE:]
    wq_rot = jnp.concatenate([jnp.zeros((q_lora, N_HEADS, QK_NOPE), F32), -r2, r1, pad], axis=2)
    wq2 = jnp.concatenate([wq_pad.reshape(q_lora, -1), wq_rot.reshape(q_lora, -1)], axis=1)

    wkv = w_ukv.reshape(kv_lora, N_HEADS, QK_NOPE + V_DIM)
    w_uk = wkv[:, :, :QK_NOPE]
    w_uv = wkv[:, :, QK_NOPE:]
    wk_pad = jnp.concatenate([w_uk, jnp.zeros((kv_lora, N_HEADS, LANES - QK_NOPE), F32)], axis=2)
    wv_pad = jnp.concatenate([w_uv, jnp.zeros((kv_lora, N_HEADS, LANES - V_DIM), F32)], axis=2)
    v_one = jnp.zeros((1, N_HEADS, LANES), F32).at[:, :, V_DIM].set(1.0)
    place = jnp.zeros((QK_ROPE, N_HEADS, LANES), F32)
    place = place.at[jnp.arange(QK_ROPE), :, QK_NOPE + jnp.arange(QK_ROPE)].set(1.0)
    wuk_t = jnp.concatenate([jnp.transpose(w_uk, (1, 2, 0)),
                             jnp.zeros((N_HEADS, LANES - QK_NOPE, kv_lora), F32)], axis=1)
    eye = jnp.eye(N_HEADS, dtype=F32)
    w_uv_bd = jnp.einsum("chv,hg->hcgv", w_uv, eye).reshape(N_HEADS * kv_lora, N_HEADS * V_DIM)
    wr_hi = w_router.astype(BF16)
    wr_lo = (w_router - wr_hi.astype(F32)).astype(BF16)
    return {
        "d_conv": d_conv, "q_lora": q_lora, "kv_lora": kv_lora,
        "g_mix": norm_mix_g.reshape(1, -1), "w_in": w_in_pad.astype(BF16), "conv_w": conv_w,
        "g_q": q_norm_g.reshape(1, -1), "g_kv": kv_norm_g.reshape(1, -1), "wq2": wq2.astype(BF16),
        "g_conv": out_norm_conv_g.reshape(1, -1), "wk_pad": wk_pad.reshape(kv_lora, -1).astype(BF16),
        "place": place.reshape(QK_ROPE, -1).astype(BF16), "wv_pad": wv_pad.reshape(kv_lora, -1).astype(BF16),
        "v_one": v_one.reshape(1, -1), "wuk_t": wuk_t.astype(BF16),
        "g_attn": out_norm_attn_g.reshape(1, -1), "wo_conv": w_out[:d_conv, :].astype(BF16),
        "wo_attn": w_out[d_conv:, :].astype(BF16), "w_uv_bd": w_uv_bd.astype(BF16),
        "g_ffn": norm_ffn_g.reshape(1, -1), "wr_hi": wr_hi, "wr_lo": wr_lo, "b_router": b_router.reshape(1, -1),
    }


def kernel(x_prompt, x_sample, cache_kv_latent, cache_k_rope, state_conv, page_table, norm_mix_g, w_in, conv_w,
           q_norm_g, kv_norm_g, w_uq, w_ukv, out_norm_conv_g, out_norm_attn_g, w_out, norm_ffn_g, w_router,
           b_router, w_gate_up, b_gate_up, w_down, b_down, final_norm_g):
    depth = w_in.shape[0]
    batch, seq, d = x_prompt.shape
    dec_b, dec_t, _ = x_sample.shape
    n_exp = w_router.shape[-1]
    past_len = page_table.shape[1] * cache_kv_latent.shape[2]
    assert batch == 1, "the prompt path lays one sequence out along rows"
    assert depth == 1, "deeper stacks need the un-normalised residual between layers"
    pos_p = jnp.arange(seq, dtype=jnp.int32)
    pos_s = jnp.tile(past_len + jnp.arange(dec_t, dtype=jnp.int32), dec_b)
    hp = x_prompt.reshape(seq, d)
    hs = x_sample.reshape(dec_b * dec_t, d)
    g_fin = final_norm_g.reshape(1, -1)
    outs = [[] for _ in range(6)]
    for l in range(depth):
        w = _prep_layer(norm_mix_g[l], w_in[l], conv_w[l], q_norm_g[l], kv_norm_g[l], w_uq[l], w_ukv[l],
                        out_norm_conv_g[l], out_norm_attn_g[l], w_out[l], norm_ffn_g[l], w_router[l], b_router[l])
        d_conv, kv_lora = w["d_conv"], w["kv_lora"]

        u_p, ycn_p, q_p, ckv_p, kr_p, k_p, v_p = _front(hp, pos_p, w, prompt=True)
        ya_p = _flash(q_p, k_p, v_p)
        hp_mid, xn_p, route_p, count_p = _back(hp, ycn_p, ya_p, w, latent_values=False)

        st = state_conv[l]
        t_idx = jnp.arange(dec_t)
        ov1 = jnp.where((t_idx == 0)[None, :, None], st[:, 1:2, :], 0.0).reshape(dec_b * dec_t, d_conv)
        ov2 = jnp.where((t_idx == 0)[None, :, None], st[:, 0:1, :],
                        jnp.where((t_idx == 1)[None, :, None], st[:, 1:2, :], 0.0)).reshape(dec_b * dec_t, d_conv)
        u_s, ycn_s, q_s, ckv_s, kr_s, qlat_s = _front(hs, pos_s, w, prompt=False, ov=(ov1, ov2), seq_rows=dec_t)
        rows = dec_t * N_HEADS
        qr_s = q_s.reshape(dec_b, rows, LANES)[:, :, QK_NOPE:QK_NOPE + QK_ROPE]
        cache_krt = jnp.swapaxes(cache_k_rope[l], 1, 2)
        o_lat = _paged_attention(qlat_s.reshape(dec_b, rows, kv_lora), qr_s,
                                 ckv_s.reshape(dec_b, dec_t, kv_lora), kr_s.reshape(dec_b, dec_t, QK_ROPE),
                                 cache_kv_latent[l], cache_krt, page_table)
        ya_s = o_lat.reshape(dec_b * dec_t, N_HEADS * kv_lora)
        hs_mid, xn_s, route_s, count_s = _back(hs, ycn_s, ya_s, w, latent_values=True)

        (slot_p, slot_s), tile_expert, next_expert, valid_rows, n_used, n_slots = _slot_tables(
            (route_p, route_s), (count_p, count_s), n_exp)
        xs = jnp.zeros((n_slots, d), F32)
        xs = _dispatch(xn_p, slot_p, xs)
        xs = _dispatch(xn_s, slot_s, xs)
        ys = _experts(xs, tile_expert, next_expert, valid_rows, n_used,
                      w_gate_up[l], b_gate_up[l], w_down[l], b_down[l])
        hp = _combine(slot_p, route_p, hp_mid, ys, g_fin)
        hs = _combine(slot_s, route_s, hs_mid, ys, g_fin)

        outs[0].append(ckv_p.reshape(batch, seq, kv_lora))
        outs[1].append(kr_p.reshape(batch, seq, QK_ROPE))
        outs[2].append(u_p[seq - 2:, :].reshape(batch, 2, d_conv))
        outs[3].append(ckv_s.reshape(dec_b, dec_t, kv_lora))
        outs[4].append(kr_s.reshape(dec_b, dec_t, QK_ROPE))
        outs[5].append(u_s.reshape(dec_b, dec_t, d_conv)[:, dec_t - 2:, :])
    return (hp.reshape(batch, seq, d), hs.reshape(dec_b, dec_t, d)) + tuple(jnp.stack(o) for o in outs)
```

```python
import functools
import math

import jax
import jax.numpy as jnp
from jax import lax
from jax.experimental import pallas as pl
from jax.experimental.pallas import tpu as pltpu

N_HEADS = 8
QK_NOPE = 64
QK_ROPE = 32
V_DIM = 64
HALF_ROPE = QK_ROPE // 2
ROPE_THETA = 10000.0
SOFTMAX_SCALE = (QK_NOPE + QK_ROPE) ** -0.5
LOG2E = math.log2(math.e)
TOP_K = 4
SWIGLU_LIMIT = 7.0
SWIGLU_ALPHA = 1.702
RMS_EPS = 1e-6
NEG = -1e30

LANES = 128
MIB = 1024 * 1024
VMEM_LIMIT = 56 * MIB
F32 = jnp.float32
BF16 = jnp.bfloat16

FRONT_ROWS = 1024
FLASH_ROWS = 1024
FLASH_TILES_PER_TRIP = 4
BACK_ROWS = 1024
BACK_SUB_ROWS = 512
PAGES_PER_CHUNK = 64
MOE_TILE = 512
DISPATCH_ROWS = 512
COMBINE_ROWS = 512
ROUTE_EXPERT, ROUTE_RANK, ROUTE_GATE = 0, TOP_K, 2 * TOP_K

NT_DIMS = (((1,), (1,)), ((), ()))


def _rms(x, g):
    return x * lax.rsqrt(jnp.mean(x * x, axis=-1, keepdims=True) + RMS_EPS) * g


def _dot(a, b):
    return jnp.dot(a, b, preferred_element_type=F32)


def _dot_nt(a, b):
    return lax.dot_general(a, b, NT_DIMS, preferred_element_type=F32)


def _params(n_axes):
    return pltpu.CompilerParams(dimension_semantics=("arbitrary",) * n_axes, vmem_limit_bytes=VMEM_LIMIT)


def _front_kernel(*refs, d_conv, q_lora, kv_lora, seq_rows, prompt):
    if prompt:
        (x_ref, cs_ref, gmix_ref, win_ref, cw_ref, gq_ref, gkv_ref, wq_ref, gconv_ref,
         wk_ref, place_ref, wv_ref, vone_ref,
         u_ref, ycn_ref, q_ref, ckv_ref, kr_ref, k_ref, v_ref, carry_ref) = refs
    else:
        (x_ref, cs_ref, ov1_ref, ov2_ref, gmix_ref, win_ref, cw_ref, gq_ref, gkv_ref, wq_ref,
         gconv_ref, wukt_ref,
         u_ref, ycn_ref, q_ref, ckv_ref, kr_ref, qlat_ref) = refs
    tm = x_ref.shape[0]
    n = _rms(x_ref[...], gmix_ref[...]).astype(BF16)
    z = _dot(n, win_ref[...])
    c0 = 3 * d_conv
    b_g = z[:, 0:d_conv]
    u = z[:, d_conv:2 * d_conv] * z[:, 2 * d_conv:c0]
    z_q = z[:, c0:c0 + q_lora]
    z_kv = z[:, c0 + q_lora:c0 + q_lora + kv_lora]
    z_r = z[:, c0 + q_lora + kv_lora:]
    u_ref[...] = u

    row = lax.broadcasted_iota(jnp.int32, u.shape, 0)
    r1 = pltpu.roll(u, 1, 0)
    r2 = pltpu.roll(u, 2, 0)
    if prompt:
        @pl.when(pl.program_id(0) == 0)
        def _():
            carry_ref[...] = jnp.zeros_like(carry_ref)
        last1 = carry_ref[7:8, :]
        last2 = carry_ref[6:7, :]
        p1 = jnp.where(row == 0, last1, r1)
        p2 = jnp.where(row == 0, last2, jnp.where(row == 1, last1, r2))
        carry_ref[...] = u[tm - 8:tm, :]
    else:
        t = row % seq_rows
        p1 = jnp.where(t == 0, ov1_ref[...], r1)
        p2 = jnp.where(t < 2, ov2_ref[...], r2)
    cw = cw_ref[...]
    y_c = p2 * cw[0:1, :] + p1 * cw[1:2, :] + u * cw[2:3, :]
    ycn_ref[...] = _rms(b_g * y_c, gconv_ref[...]).astype(BF16)

    nq = _rms(z_q, gq_ref[...]).astype(BF16)
    q2 = _dot(nq, wq_ref[...])
    hw = N_HEADS * LANES
    cs = cs_ref[...]
    cos, sin = cs[:, :HALF_ROPE], cs[:, HALF_ROPE:]
    qs = SOFTMAX_SCALE * LOG2E
    pad = jnp.zeros((tm, LANES - QK_NOPE - QK_ROPE), F32)
    tqc = jnp.concatenate([jnp.full((tm, QK_NOPE), qs, F32), cos * qs, cos * qs, pad], axis=1)
    tqs = jnp.concatenate([jnp.zeros((tm, QK_NOPE), F32), sin * qs, sin * qs, pad], axis=1)
    q_heads = []
    for h in range(N_HEADS):
        qa = q2[:, h * LANES:(h + 1) * LANES]
        qb = q2[:, hw + h * LANES:hw + (h + 1) * LANES]
        q_heads.append((qa * tqc + qb * tqs).astype(BF16))
    q_ref[...] = jnp.concatenate(q_heads, axis=1)

    ckv = _rms(z_kv, gkv_ref[...])
    ckv_ref[...] = ckv
    tk = jnp.concatenate([cos, cos, sin, sin, jnp.zeros((tm, LANES - 2 * QK_ROPE), F32)], axis=1)
    zr = z_r * tk
    kr = zr[:, 0:QK_ROPE] + zr[:, QK_ROPE:2 * QK_ROPE]
    kr_ref[...] = kr

    if prompt:
        ckv_b = ckv.astype(BF16)
        k_ref[...] = (_dot(ckv_b, wk_ref[...]) + _dot(kr.astype(BF16), place_ref[...])).astype(BF16)
        v_ref[...] = (_dot(ckv_b, wv_ref[...]) + vone_ref[...]).astype(BF16)
    else:
        for h in range(N_HEADS):
            qlat_ref[:, h * kv_lora:(h + 1) * kv_lora] = _dot(q_heads[h], wukt_ref[h]).astype(BF16)


def _front(x, pos, w, *, prompt, ov=None, seq_rows=0):
    n, d = x.shape
    d_conv, q_lora, kv_lora = w["d_conv"], w["q_lora"], w["kv_lora"]
    tm = min(FRONT_ROWS, n)
    hw = N_HEADS * LANES
    inv_freq = ROPE_THETA ** (-jnp.arange(HALF_ROPE, dtype=F32) / HALF_ROPE)
    ang = pos.astype(F32)[:, None] * inv_freq[None, :]
    cs = jnp.concatenate([jnp.cos(ang), jnp.sin(ang)], axis=1)

    row = lambda c: pl.BlockSpec((tm, c), lambda i: (i, 0))
    full = lambda a: pl.BlockSpec(a.shape, lambda i: (0,) * a.ndim)
    ins = [x, cs]
    in_specs = [row(d), row(2 * HALF_ROPE)]
    if not prompt:
        ins += [ov[0], ov[1]]
        in_specs += [row(d_conv), row(d_conv)]
    shared = [w["g_mix"], w["w_in"], w["conv_w"], w["g_q"], w["g_kv"], w["wq2"], w["g_conv"]]
    shared += [w["wk_pad"], w["place"], w["wv_pad"], w["v_one"]] if prompt else [w["wuk_t"]]
    ins += shared
    in_specs += [full(a) for a in shared]

    outs = [jax.ShapeDtypeStruct((n, d_conv), F32), jax.ShapeDtypeStruct((n, d_conv), BF16),
            jax.ShapeDtypeStruct((n, hw), BF16), jax.ShapeDtypeStruct((n, kv_lora), F32),
            jax.ShapeDtypeStruct((n, QK_ROPE), F32)]
    out_specs = [row(d_conv), row(d_conv), row(hw), row(kv_lora), row(QK_ROPE)]
    scratch = []
    if prompt:
        outs += [jax.ShapeDtypeStruct((n, hw), BF16), jax.ShapeDtypeStruct((n, hw), BF16)]
        out_specs += [row(hw), row(hw)]
        scratch = [pltpu.VMEM((8, d_conv), F32)]
    else:
        outs += [jax.ShapeDtypeStruct((n, N_HEADS * kv_lora), BF16)]
        out_specs += [row(N_HEADS * kv_lora)]
    kern = functools.partial(_front_kernel, d_conv=d_conv, q_lora=q_lora, kv_lora=kv_lora, seq_rows=seq_rows,
                             prompt=prompt)
    return pl.pallas_call(
        kern, out_shape=outs, grid=(n // tm,), in_specs=in_specs, out_specs=out_specs, scratch_shapes=scratch,
        name="front_prompt" if prompt else "front_sample", compiler_params=_params(1),
    )(*ins)


def _flash_kernel(q_ref, k_ref, v_ref, o_ref, m_sc, acc_sc, *, tq):
    i = pl.program_id(1)
    m_sc[...] = jnp.full_like(m_sc, NEG)
    acc_sc[...] = jnp.zeros_like(acc_sc)
    half = tq // 2

    def block(row0, n_rows, key0, n_keys, diagonal):
        rows = slice(row0, row0 + n_rows)
        for h in range(2):
            sl = slice(h * LANES, (h + 1) * LANES)
            s = _dot_nt(q_ref[rows, sl], k_ref[pl.ds(key0, n_keys), sl])
            if diagonal:
                r = lax.broadcasted_iota(jnp.int32, s.shape, 0)
                c = lax.broadcasted_iota(jnp.int32, s.shape, 1)
                s = jnp.where(c <= r, s, NEG)
            m_prev = m_sc[h, rows]
            m_new = jnp.maximum(m_prev, jnp.max(s, axis=-1, keepdims=True))
            alpha = jnp.exp2(m_prev - m_new)
            p = jnp.exp2(s - jnp.concatenate([m_new] * (n_keys // LANES), axis=1))
            acc_sc[h, rows] = alpha * acc_sc[h, rows] + _dot(p.astype(BF16), v_ref[pl.ds(key0, n_keys), sl])
            m_sc[h, rows] = m_new

    def full_block(j):
        block(0, tq, pl.multiple_of(j * tq, tq), tq, False)

    per_trip = FLASH_TILES_PER_TRIP

    def body(jj, carry):
        for t in range(per_trip):
            full_block(per_trip * jj + t)
        return carry

    lax.fori_loop(0, i // per_trip, body, 0)
    done = (i // per_trip) * per_trip
    part = per_trip // 2
    while part >= 1:
        @pl.when((i - done) & part != 0)
        def _(done=done, part=part):
            for t in range(part):
                full_block(done + t)
        done = done + ((i - done) & part)
        part //= 2

    diag = pl.multiple_of(i * tq, tq)
    block(0, tq, diag, half, True)
    block(half, half, diag + half, half, True)
    outs = []
    for h in range(2):
        a = acc_sc[h]
        outs.append(a[:, :V_DIM] / a[:, V_DIM:V_DIM + 1])
    o_ref[...] = jnp.concatenate(outs, axis=1)


def _flash(q, k, v):
    s = q.shape[0]
    tq = min(FLASH_ROWS, s)
    pairs = N_HEADS // 2
    return pl.pallas_call(
        functools.partial(_flash_kernel, tq=tq),
        out_shape=jax.ShapeDtypeStruct((s, N_HEADS * V_DIM), F32),
        grid=(pairs, s // tq),
        in_specs=[pl.BlockSpec((tq, 2 * LANES), lambda p, i: (i, p)),
                  pl.BlockSpec((s, 2 * LANES), lambda p, i: (0, p)),
                  pl.BlockSpec((s, 2 * LANES), lambda p, i: (0, p))],
        out_specs=pl.BlockSpec((tq, 2 * V_DIM), lambda p, i: (i, p)),
        scratch_shapes=[pltpu.VMEM((2, tq, LANES), F32), pltpu.VMEM((2, tq, LANES), F32)],
        name="flash_prompt", compiler_params=_params(2),
    )(q, k, v)


def _paged_kernel(pt_ref, qlat_ref, qr_ref, cnew_ref, krnew_ref, cache_c, cache_krt, o_ref, cbuf, krbuf, cbf, sem,
                  *, nb, n_pages, page, ppc, t_new):
    b = pl.program_id(0)
    slot = b % 2
    n_chunks = n_pages // ppc
    ck = ppc * page

    def fetch(bb, sl):
        for pg_i in range(n_pages):
            pg = pt_ref[bb, pg_i]
            ci, p = divmod(pg_i, ppc)
            pltpu.make_async_copy(cache_c.at[pg], cbuf.at[sl, pg_i], sem.at[sl]).start()
            pltpu.make_async_copy(cache_krt.at[pg], krbuf.at[sl, ci, :, pl.ds(p * page, page)], sem.at[sl]).start()

    @pl.when(b == 0)
    def _():
        fetch(0, 0)

    if nb > 1:
        @pl.when(b + 1 < nb)
        def _():
            fetch(b + 1, 1 - slot)

    pltpu.make_async_copy(cache_c.at[pl.ds(0, n_pages)], cbuf.at[slot], sem.at[slot]).wait()
    pltpu.make_async_copy(krbuf.at[1 - slot], krbuf.at[slot], sem.at[slot]).wait()

    qlat = qlat_ref[...]
    qr = qr_ref[...]
    rows, kv_lora = qlat.shape

    def scores(ci):
        cb = cbuf[slot, pl.ds(ci * ppc, ppc)].reshape(ck, kv_lora).astype(BF16)
        cbf[ci % 2] = cb
        return _dot_nt(qlat, cb) + _dot(qr, krbuf[slot, ci].astype(BF16))

    def absorb(ci, s, m_prev, l_prev, acc):
        m_new = jnp.maximum(m_prev, jnp.max(s, axis=-1, keepdims=True))
        alpha = jnp.exp2(m_prev - m_new)
        p_ = jnp.exp2(s - m_new)
        l_new = alpha * l_prev + jnp.sum(p_, axis=-1, keepdims=True)
        return m_new, l_new, alpha * acc + _dot(p_.astype(BF16), cbf[ci % 2])

    def chunk(ci, carry):
        s, m_prev, l_prev, acc = carry
        s_next = scores(ci + 1)
        return (s_next,) + absorb(ci, s, m_prev, l_prev, acc)

    init = (scores(0), jnp.full((rows, 1), NEG, F32), jnp.zeros((rows, 1), F32), jnp.zeros((rows, kv_lora), F32))
    s_last, m_old, l_old, acc = lax.fori_loop(0, n_chunks - 1, chunk, init)
    m_old, l_old, acc = absorb(n_chunks - 1, s_last, m_old, l_old, acc)

    ql = qlat.astype(F32)
    qrf = qr.astype(F32)
    cn = cnew_ref[...]
    kn = krnew_ref[...]
    t_row = lax.broadcasted_iota(jnp.int32, (rows, 1), 0) // N_HEADS
    s_new = []
    for j in range(t_new):
        sj = (jnp.sum(ql * cn[j:j + 1, :], axis=-1, keepdims=True)
              + jnp.sum(qrf * kn[j:j + 1, :], axis=-1, keepdims=True))
        s_new.append(jnp.where(t_row >= j, sj, NEG))
    m_fin = m_old
    for sj in s_new:
        m_fin = jnp.maximum(m_fin, sj)
    a = jnp.exp2(m_old - m_fin)
    l_fin = a * l_old
    acc = a * acc
    for j, sj in enumerate(s_new):
        pj = jnp.exp2(sj - m_fin)
        l_fin = l_fin + pj
        acc = acc + pj * cn[j:j + 1, :]
    o_ref[...] = acc / l_fin


def _paged_attention(qlat, qr, c_new, kr_new, cache_c, cache_krt, page_table):
    nb, rows, kv_lora = qlat.shape
    n_pages = page_table.shape[1]
    page = cache_c.shape[1]
    t_new = c_new.shape[1]
    ppc = math.gcd(PAGES_PER_CHUNK, n_pages)
    per_b = lambda r, w: pl.BlockSpec((None, r, w), lambda b, pt: (b, 0, 0))
    any_spec = pl.BlockSpec(memory_space=pl.ANY)
    grid_spec = pltpu.PrefetchScalarGridSpec(
        num_scalar_prefetch=1, grid=(nb,),
        in_specs=[per_b(rows, kv_lora), per_b(rows, QK_ROPE), per_b(t_new, kv_lora), per_b(t_new, QK_ROPE),
                  any_spec, any_spec],
        out_specs=per_b(rows, kv_lora),
        scratch_shapes=[pltpu.VMEM((2, n_pages, page, kv_lora), F32),
                        pltpu.VMEM((2, n_pages // ppc, QK_ROPE, ppc * page), F32),
                        pltpu.VMEM((2, ppc * page, kv_lora), BF16),
                        pltpu.SemaphoreType.DMA((2,))])
    return pl.pallas_call(
        functools.partial(_paged_kernel, nb=nb, n_pages=n_pages, page=page, ppc=ppc, t_new=t_new),
        out_shape=jax.ShapeDtypeStruct((nb, rows, kv_lora), F32), grid_spec=grid_spec, name="paged_sample",
        compiler_params=_params(1),
    )(page_table, qlat, qr, c_new, kr_new, cache_c, cache_krt)


def _back_kernel(*refs, latent_values, sub):
    if latent_values:
        (h_ref, ycn_ref, ya_ref, wuv_ref, gattn_ref, woc_ref, woa_ref, gffn_ref, wrh_ref, wrl_ref, br_ref, tri_ref,
         hp_ref, xn_ref, route_ref, count_ref, base_sc) = refs
    else:
        (h_ref, ycn_ref, ya_ref, gattn_ref, woc_ref, woa_ref, gffn_ref, wrh_ref, wrl_ref, br_ref, tri_ref,
         hp_ref, xn_ref, route_ref, count_ref, base_sc) = refs

    @pl.when(pl.program_id(0) == 0)
    def _():
        base_sc[...] = jnp.zeros_like(base_sc)

    def sub_tile(rows, base):
        if latent_values:
            ya = _dot(ya_ref[rows, :].astype(BF16), wuv_ref[...])
        else:
            ya = ya_ref[rows, :]
        yan = _rms(ya, gattn_ref[...]).astype(BF16)
        hp = h_ref[rows, :] + _dot(ycn_ref[rows, :], woc_ref[...]) + _dot(yan, woa_ref[...])
        hp_ref[rows, :] = hp
        xn = _rms(hp, gffn_ref[...])
        xn_ref[rows, :] = xn
        x_hi = xn.astype(BF16)
        x_lo = (xn - x_hi.astype(F32)).astype(BF16)
        logits = _dot(x_hi, wrh_ref[...]) + _dot(x_hi, wrl_ref[...]) + _dot(x_lo, wrh_ref[...]) + br_ref[...]
        n_exp = logits.shape[1]
        lane = lax.broadcasted_iota(jnp.int32, logits.shape, 1).astype(F32)
        work = logits
        picks = []
        for _ in range(TOP_K):
            vmax = jnp.max(work, axis=-1, keepdims=True)
            idx = jnp.min(jnp.where(work == vmax, lane, float(n_exp)), axis=-1, keepdims=True)
            sel = lane == idx
            picks.append((vmax, idx, sel))
            work = jnp.where(sel, -jnp.inf, work)
        top = picks[0][0]
        exps = [jnp.exp(v - top) for v, _, _ in picks]
        denom = exps[0]
        for e in exps[1:]:
            denom = denom + e

        onehot = jnp.zeros_like(logits)
        for _, _, sel in picks:
            onehot = jnp.where(sel, 1.0, onehot)
        rank_all = _dot(tri_ref[...], onehot.astype(BF16)) + base

        lane_r = lax.broadcasted_iota(jnp.int32, (sub, LANES), 1)
        route = jnp.zeros((sub, LANES), F32)
        for k, (e, (_, idx, sel)) in enumerate(zip(exps, picks)):
            rank_k = jnp.sum(jnp.where(sel, rank_all, 0.0), axis=-1, keepdims=True)
            route = jnp.where(lane_r == ROUTE_EXPERT + k, idx, route)
            route = jnp.where(lane_r == ROUTE_RANK + k, rank_k, route)
            route = jnp.where(lane_r == ROUTE_GATE + k, e / denom, route)
        route_ref[rows, :] = route
        return base + jnp.sum(onehot, axis=0, keepdims=True)

    base = base_sc[...]
    for s in range(h_ref.shape[0] // sub):
        base = sub_tile(slice(s * sub, (s + 1) * sub), base)
    base_sc[...] = base
    count_ref[...] = base


def _back(h, ycn, ya, w, *, latent_values):
    n, d = h.shape
    tm = min(BACK_ROWS, n)
    sub = min(BACK_SUB_ROWS, tm)
    n_exp = w["wr_hi"].shape[1]
    row = lambda c: pl.BlockSpec((tm, c), lambda i: (i, 0))
    full = lambda a: pl.BlockSpec(a.shape, lambda i: (0,) * a.ndim)
    tri = jnp.tril(jnp.ones((sub, sub), F32), k=-1).astype(BF16)
    shared = [w["g_attn"], w["wo_conv"], w["wo_attn"], w["g_ffn"], w["wr_hi"], w["wr_lo"], w["b_router"], tri]
    if latent_values:
        shared = [w["w_uv_bd"]] + shared
    return pl.pallas_call(
        functools.partial(_back_kernel, latent_values=latent_values, sub=sub),
        out_shape=[jax.ShapeDtypeStruct((n, d), F32), jax.ShapeDtypeStruct((n, d), F32),
                   jax.ShapeDtypeStruct((n, LANES), F32), jax.ShapeDtypeStruct((1, n_exp), F32)],
        grid=(n // tm,),
        in_specs=[row(d), row(ycn.shape[1]), row(ya.shape[1])] + [full(a) for a in shared],
        out_specs=[row(d), row(d), row(LANES), pl.BlockSpec((1, n_exp), lambda i: (0, 0))],
        scratch_shapes=[pltpu.VMEM((1, n_exp), F32)],
        name="back_sample" if latent_values else "back_prompt", compiler_params=_params(1),
    )(h, ycn, ya, *shared)


def _dispatch_kernel(slot_ref, x_ref, xs_in_ref, xs_ref, sem):
    del xs_in_ref
    tm = x_ref.shape[0]

    def body(r, carry):
        for k in range(TOP_K):
            s = slot_ref[r * TOP_K + k]
            pltpu.make_async_copy(x_ref.at[pl.ds(r, 1)], xs_ref.at[pl.ds(s, 1)], sem).start(priority=k % 2)
        return carry

    lax.fori_loop(0, tm, body, 0, unroll=4)
    for k in range(TOP_K):
        pltpu.make_async_copy(x_ref, xs_ref.at[pl.ds(0, tm)], sem).wait()


def _dispatch(x, slots, xs):
    n, d = x.shape
    tm = min(DISPATCH_ROWS, n)
    return pl.pallas_call(
        _dispatch_kernel,
        out_shape=jax.ShapeDtypeStruct(xs.shape, F32),
        grid=(n // tm,),
        in_specs=[pl.BlockSpec((tm * TOP_K,), lambda i: (i,), memory_space=pltpu.SMEM),
                  pl.BlockSpec((tm, d), lambda i: (i, 0)),
                  pl.BlockSpec(memory_space=pl.ANY)],
        out_specs=pl.BlockSpec(memory_space=pl.ANY),
        scratch_shapes=[pltpu.SemaphoreType.DMA(())],
        input_output_aliases={2: 0},
        name="moe_dispatch", compiler_params=_params(1),
    )(slots, x, xs)


def _expert_kernel(te_ref, nx_ref, vr_ref, nu_ref, xs_ref, bgu_ref, bd_ref, wgu_hbm, wd_hbm, ys_ref,
                   wgu_st, wd_st, wgu_bf, wd_bf, sem):
    ti = pl.program_id(0)
    used = ti < nu_ref[0]
    new_expert = (ti == 0) | (te_ref[ti] != te_ref[jnp.maximum(ti - 1, 0)])
    d_ff = wd_bf.shape[0]

    def weight_copies(e):
        return (pltpu.make_async_copy(wgu_hbm.at[e], wgu_st, sem.at[0]),
                pltpu.make_async_copy(wd_hbm.at[e], wd_st, sem.at[1]))

    @pl.when(ti == 0)
    def _():
        for cp in weight_copies(te_ref[0]):
            cp.start()

    @pl.when(used & new_expert)
    def _():
        for cp in weight_copies(te_ref[ti]):
            cp.wait()
        wgu_bf[...] = wgu_st[...].astype(BF16)
        wd_bf[...] = wd_st[...].astype(BF16)

        @pl.when(nx_ref[ti] >= 0)
        def _():
            for cp in weight_copies(nx_ref[ti]):
                cp.start()

    def mlp(x):
        x = x.astype(BF16)
        bgu = bgu_ref[...]
        gate = jnp.minimum(_dot(x, wgu_bf[:, :d_ff]) + bgu[:, :d_ff], SWIGLU_LIMIT)
        up = jnp.clip(_dot(x, wgu_bf[:, d_ff:]) + bgu[:, d_ff:], -SWIGLU_LIMIT, SWIGLU_LIMIT)
        hid = (up + 1.0) * gate * jax.nn.sigmoid(SWIGLU_ALPHA * gate)
        return _dot(hid.astype(BF16), wd_bf[...]) + bd_ref[...]

    half = xs_ref.shape[0] // 2
    valid = vr_ref[ti]

    @pl.when(used & (valid > half))
    def _():
        ys_ref[...] = mlp(xs_ref[...])

    @pl.when(used & (valid <= half))
    def _():
        ys_ref[:half, :] = mlp(xs_ref[:half, :])
        ys_ref[half:, :] = jnp.zeros((half, ys_ref.shape[1]), F32)

    @pl.when(jnp.logical_not(used))
    def _():
        ys_ref[...] = jnp.zeros_like(ys_ref)


def _experts(xs, tile_expert, next_expert, valid_rows, n_used, w_gu, b_gu, w_d, b_d):
    n_slots, d = xs.shape
    n_exp, _, d_ff2 = w_gu.shape
    d_ff = d_ff2 // 2
    tmm = MOE_TILE
    rows = lambda t, te, nx, vr, nu: (jnp.minimum(t, nu[0] - 1), 0)
    any_spec = pl.BlockSpec(memory_space=pl.ANY)
    grid_spec = pltpu.PrefetchScalarGridSpec(
        num_scalar_prefetch=4, grid=(n_slots // tmm,),
        in_specs=[pl.BlockSpec((tmm, d), rows),
                  pl.BlockSpec((None, 1, d_ff2), lambda t, te, nx, vr, nu: (te[t], 0, 0)),
                  pl.BlockSpec((None, 1, d), lambda t, te, nx, vr, nu: (te[t], 0, 0)),
                  any_spec, any_spec],
        out_specs=pl.BlockSpec((tmm, d), lambda t, te, nx, vr, nu: (t, 0)),
        scratch_shapes=[pltpu.VMEM((d, d_ff2), F32), pltpu.VMEM((d_ff, d), F32),
                        pltpu.VMEM((d, d_ff2), BF16), pltpu.VMEM((d_ff, d), BF16),
                        pltpu.SemaphoreType.DMA((2,))])
    return pl.pallas_call(
        _expert_kernel, out_shape=jax.ShapeDtypeStruct((n_slots, d), F32), grid_spec=grid_spec,
        name="moe_experts", compiler_params=_params(1),
    )(tile_expert, next_expert, valid_rows, n_used, xs, b_gu.reshape(n_exp, 1, d_ff2), b_d.reshape(n_exp, 1, d),
      w_gu, w_d)


def _combine_kernel(slot_ref, next_slot_ref, route_ref, hp_ref, ys_ref, gfin_ref, o_ref, buf, sem, *, steps):
    i = pl.program_id(0)
    tm = hp_ref.shape[0]

    def gather(slots, b):
        def body(r, carry):
            for k in range(TOP_K):
                s = slots[r * TOP_K + k]
                pltpu.make_async_copy(ys_ref.at[pl.ds(s, 1)], buf.at[b, k, pl.ds(r, 1)],
                                      sem.at[b]).start(priority=k % 2)
            return carry

        lax.fori_loop(0, tm, body, 0, unroll=4)

    @pl.when(i == 0)
    def _():
        gather(slot_ref, 0)

    if steps > 1:
        @pl.when(i + 1 < steps)
        def _():
            gather(next_slot_ref, (i + 1) % 2)

    b = i % 2
    for k in range(TOP_K):
        pltpu.make_async_copy(ys_ref.at[pl.ds(0, tm)], buf.at[b, k], sem.at[b]).wait()
    route = route_ref[...]
    acc = hp_ref[...]
    for k in range(TOP_K):
        acc = acc + route[:, ROUTE_GATE + k:ROUTE_GATE + k + 1] * buf[b, k]
    o_ref[...] = _rms(acc, gfin_ref[...])


def _combine(slots, route, hp, ys, g_fin):
    n, d = hp.shape
    tm = min(COMBINE_ROWS, n)
    steps = n // tm
    return pl.pallas_call(
        functools.partial(_combine_kernel, steps=steps),
        out_shape=jax.ShapeDtypeStruct((n, d), F32),
        grid=(steps,),
        in_specs=[pl.BlockSpec((tm * TOP_K,), lambda i: (i,), memory_space=pltpu.SMEM),
                  pl.BlockSpec((tm * TOP_K,), lambda i: (jnp.minimum(i + 1, steps - 1),), memory_space=pltpu.SMEM),
                  pl.BlockSpec((tm, LANES), lambda i: (i, 0)),
                  pl.BlockSpec((tm, d), lambda i: (i, 0)),
                  pl.BlockSpec(memory_space=pl.ANY),
                  pl.BlockSpec((1, d), lambda i: (0, 0))],
        out_specs=pl.BlockSpec((tm, d), lambda i: (i, 0)),
        scratch_shapes=[pltpu.VMEM((2, TOP_K, tm, d), F32), pltpu.SemaphoreType.DMA((2,))],
        name="moe_combine", compiler_params=_params(1),
    )(slots, slots, route, hp, ys, g_fin)


def _slot_tables(routes, counts, n_exp):
    tmm = MOE_TILE
    counts = [c.reshape(n_exp).astype(jnp.int32) for c in counts]
    total = sum(counts)
    group = ((total + tmm - 1) // tmm) * tmm
    ends = jnp.cumsum(group)
    start = ends - group
    slots = []
    before = jnp.zeros_like(total)
    experts = jnp.arange(n_exp, dtype=jnp.int32)
    for route, cnt in zip(routes, counts):
        eid = route[:, ROUTE_EXPERT:ROUTE_EXPERT + TOP_K].astype(jnp.int32)
        rank = route[:, ROUTE_RANK:ROUTE_RANK + TOP_K].astype(jnp.int32)
        first = jnp.sum(jnp.where(eid[:, :, None] == experts, start + before, 0), axis=-1)
        slots.append((first + rank).reshape(-1))
        before = before + cnt
    n_tokens = sum(r.shape[0] for r in routes)
    max_tiles = (n_tokens * TOP_K) // tmm + n_exp
    n_used = ends[-1] // tmm
    tile_start = jnp.minimum(jnp.arange(max_tiles, dtype=jnp.int32) * tmm, ends[-1] - 1)
    tile_expert = jnp.sum((ends[None, :] <= tile_start[:, None]).astype(jnp.int32), axis=1)
    group_end = jnp.sum(jnp.where(tile_expert[:, None] == experts, ends, 0), axis=1)
    following = jnp.sum((ends[None, :] <= group_end[:, None]).astype(jnp.int32), axis=1)
    next_expert = jnp.where(group_end < ends[-1], following, -1).astype(jnp.int32)
    filled_end = jnp.sum(jnp.where(tile_expert[:, None] == experts, start + total, 0), axis=1)
    valid_rows = jnp.clip(filled_end - jnp.arange(max_tiles, dtype=jnp.int32) * tmm, 0, tmm).astype(jnp.int32)
    return (slots, tile_expert, next_expert, valid_rows, n_used.reshape(1).astype(jnp.int32), max_tiles * tmm)


def _prep_layer(norm_mix_g, w_in, conv_w, q_norm_g, kv_norm_g, w_uq, w_ukv, out_norm_conv_g, out_norm_attn_g,
                w_out, norm_ffn_g, w_router, b_router):
    d = w_in.shape[0]
    d_conv = conv_w.shape[1]
    q_lora = q_norm_g.shape[0]
    kv_lora = kv_norm_g.shape[0]
    c_kr = 3 * d_conv + q_lora + kv_lora
    w_kr = w_in[:, c_kr:c_kr + QK_ROPE]
    w_kr_rot = jnp.concatenate([-w_kr[:, HALF_ROPE:], w_kr[:, :HALF_ROPE]], axis=1)
    w_in_pad = jnp.concatenate([w_in[:, :c_kr], w_kr, w_kr_rot, jnp.zeros((d, LANES - 2 * QK_ROPE), F32)], axis=1)

    qk = QK_NOPE + QK_ROPE
    wq = w_uq.reshape(q_lora, N_HEADS, qk)
    pad = jnp.zeros((q_lora, N_HEADS, LANES - qk), F32)
    wq_pad = jnp.concatenate([wq, pad], axis=2)
    r1 = wq[:, :, QK_NOPE:QK_NOPE + HALF_ROPE]
    r2 = wq[:, :, QK_NOPE + HALF_ROPE:]
    wq_rot = jnp.concatenate([jnp.zeros((q_lora, N_HEADS, QK_NOPE), F32), -r2, r1, pad], axis=2)
    wq2 = jnp.concatenate([wq_pad.reshape(q_lora, -1), wq_rot.reshape(q_lora, -1)], axis=1)

    wkv = w_ukv.reshape(kv_lora, N_HEADS, QK_NOPE + V_DIM)
    w_uk = wkv[:, :, :QK_NOPE]
    w_uv = wkv[:, :, QK_NOPE:]
    wk_pad = jnp.concatenate([w_uk, jnp.zeros((kv_lora, N_HEADS, LANES - QK_NOPE), F32)], axis=2)
    wv_pad = jnp.concatenate([w_uv, jnp.zeros((kv_lora, N_HEADS, LANES - V_DIM), F32)], axis=2)
    v_one = jnp.zeros((1, N_HEADS, LANES), F32).at[:, :, V_DIM].set(1.0)
    place = jnp.zeros((QK_ROPE, N_HEADS, LANES), F32)
    place = place.at[jnp.arange(QK_ROPE), :, QK_NOPE + jnp.arange(QK_ROPE)].set(1.0)
    wuk_t = jnp.concatenate([jnp.transpose(w_uk, (1, 2, 0)),
                             jnp.zeros((N_HEADS, LANES - QK_NOPE, kv_lora), F32)], axis=1)
    eye = jnp.eye(N_HEADS, dtype=F32)
    w_uv_bd = jnp.einsum("chv,hg->hcgv", w_uv, eye).reshape(N_HEADS * kv_lora, N_HEADS * V_DIM)
    wr_hi = w_router.astype(BF16)
    wr_lo = (w_router - wr_hi.astype(F32)).astype(BF16)
    return {
        "d_conv": d_conv, "q_lora": q_lora, "kv_lora": kv_lora,
        "g_mix": norm_mix_g.reshape(1, -1), "w_in": w_in_pad.astype(BF16), "conv_w": conv_w,
        "g_q": q_norm_g.reshape(1, -1), "g_kv": kv_norm_g.reshape(1, -1), "wq2": wq2.astype(BF16),
        "g_conv": out_norm_conv_g.reshape(1, -1), "wk_pad": wk_pad.reshape(kv_lora, -1).astype(BF16),
        "place": place.reshape(QK_ROPE, -1).astype(BF16), "wv_pad": wv_pad.reshape(kv_lora, -1).astype(BF16),
        "v_one": v_one.reshape(1, -1), "wuk_t": wuk_t.astype(BF16),
        "g_attn": out_norm_attn_g.reshape(1, -1), "wo_conv": w_out[:d_conv, :].astype(BF16),
        "wo_attn": w_out[d_conv:, :].astype(BF16), "w_uv_bd": w_uv_bd.astype(BF16),
        "g_ffn": norm_ffn_g.reshape(1, -1), "wr_hi": wr_hi, "wr_lo": wr_lo, "b_router": b_router.reshape(1, -1),
    }


def kernel(x_prompt, x_sample, cache_kv_latent, cache_k_rope, state_conv, page_table, norm_mix_g, w_in, conv_w,
           q_norm_g, kv_norm_g, w_uq, w_ukv, out_norm_conv_g, out_norm_attn_g, w_out, norm_ffn_g, w_router,
           b_router, w_gate_up, b_gate_up, w_down, b_down, final_norm_g):
    depth = w_in.shape[0]
    batch, seq, d = x_prompt.shape
    dec_b, dec_t, _ = x_sample.shape
    n_exp = w_router.shape[-1]
    past_len = page_table.shape[1] * cache_kv_latent.shape[2]
    assert batch == 1, "the prompt path lays one sequence out along rows"
    assert depth == 1, "deeper stacks need the un-normalised residual between layers"
    pos_p = jnp.arange(seq, dtype=jnp.int32)
    pos_s = jnp.tile(past_len + jnp.arange(dec_t, dtype=jnp.int32), dec_b)
    hp = x_prompt.reshape(seq, d)
    hs = x_sample.reshape(dec_b * dec_t, d)
    g_fin = final_norm_g.reshape(1, -1)
    outs = [[] for _ in range(6)]
    for l in range(depth):
        w = _prep_layer(norm_mix_g[l], w_in[l], conv_w[l], q_norm_g[l], kv_norm_g[l], w_uq[l], w_ukv[l],
                        out_norm_conv_g[l], out_norm_attn_g[l], w_out[l], norm_ffn_g[l], w_router[l], b_router[l])
        d_conv, kv_lora = w["d_conv"], w["kv_lora"]

        u_p, ycn_p, q_p, ckv_p, kr_p, k_p, v_p = _front(hp, pos_p, w, prompt=True)
        ya_p = _flash(q_p, k_p, v_p)
        hp_mid, xn_p, route_p, count_p = _back(hp, ycn_p, ya_p, w, latent_values=False)

        st = state_conv[l]
        t_idx = jnp.arange(dec_t)
        ov1 = jnp.where((t_idx == 0)[None, :, None], st[:, 1:2, :], 0.0).reshape(dec_b * dec_t, d_conv)
        ov2 = jnp.where((t_idx == 0)[None, :, None], st[:, 0:1, :],
                        jnp.where((t_idx == 1)[None, :, None], st[:, 1:2, :], 0.0)).reshape(dec_b * dec_t, d_conv)
        u_s, ycn_s, q_s, ckv_s, kr_s, qlat_s = _front(hs, pos_s, w, prompt=False, ov=(ov1, ov2), seq_rows=dec_t)
        rows = dec_t * N_HEADS
        qr_s = q_s.reshape(dec_b, rows, LANES)[:, :, QK_NOPE:QK_NOPE + QK_ROPE]
        cache_krt = jnp.swapaxes(cache_k_rope[l], 1, 2)
        o_lat = _paged_attention(qlat_s.reshape(dec_b, rows, kv_lora), qr_s,
                                 ckv_s.reshape(dec_b, dec_t, kv_lora), kr_s.reshape(dec_b, dec_t, QK_ROPE),
                                 cache_kv_latent[l], cache_krt, page_table)
        ya_s = o_lat.reshape(dec_b * dec_t, N_HEADS * kv_lora)
        hs_mid, xn_s, route_s, count_s = _back(hs, ycn_s, ya_s, w, latent_values=True)

        (slot_p, slot_s), tile_expert, next_expert, valid_rows, n_used, n_slots = _slot_tables(
            (route_p, route_s), (count_p, count_s), n_exp)
        xs = jnp.zeros((n_slots, d), F32)
        xs = _dispatch(xn_p, slot_p, xs)
        xs = _dispatch(xn_s, slot_s, xs)
        ys = _experts(xs, tile_expert, next_expert, valid_rows, n_used,
                      w_gate_up[l], b_gate_up[l], w_down[l], b_down[l])
        hp = _combine(slot_p, route_p, hp_mid, ys, g_fin)
        hs = _combine(slot_s, route_s, hs_mid, ys, g_fin)

        outs[0].append(ckv_p.reshape(batch, seq, kv_lora))
        outs[1].append(kr_p.reshape(batch, seq, QK_ROPE))
        outs[2].append(u_p[seq - 2:, :].reshape(batch, 2, d_conv))
        outs[3].append(ckv_s.reshape(dec_b, dec_t, kv_lora))
        outs[4].append(kr_s.reshape(dec_b, dec_t, QK_ROPE))
        outs[5].append(u_s.reshape(dec_b, dec_t, d_conv)[:, dec_t - 2:, :])
    return (hp.reshape(batch, seq, d), hs.reshape(dec_b, dec_t, d)) + tuple(jnp.stack(o) for o in outs)
```

```python
import functools
import math

import jax
import jax.numpy as jnp
from jax import lax
from jax.experimental import pallas as pl
from jax.experimental.pallas import tpu as pltpu

N_HEADS = 8
QK_NOPE = 64
QK_ROPE = 32
V_DIM = 64
HALF_ROPE = QK_ROPE // 2
ROPE_THETA = 10000.0
SOFTMAX_SCALE = (QK_NOPE + QK_ROPE) ** -0.5
LOG2E = math.log2(math.e)
TOP_K = 4
SWIGLU_LIMIT = 7.0
SWIGLU_ALPHA = 1.702
RMS_EPS = 1e-6
NEG = -1e30

LANES = 128
MIB = 1024 * 1024
VMEM_LIMIT = 56 * MIB
F32 = jnp.float32
BF16 = jnp.bfloat16

FRONT_ROWS = 1024
FLASH_ROWS = 1024
FLASH_TILES_PER_TRIP = 4
BACK_ROWS = 1024
BACK_SUB_ROWS = 512
PAGES_PER_CHUNK = 64
MOE_TILE = 512
DISPATCH_ROWS = 512
COMBINE_ROWS = 512
ROW_COPY_UNROLL = 16
ROUTE_EXPERT, ROUTE_RANK, ROUTE_GATE = 0, TOP_K, 2 * TOP_K

NT_DIMS = (((1,), (1,)), ((), ()))


def _rms(x, g):
    return x * lax.rsqrt(jnp.mean(x * x, axis=-1, keepdims=True) + RMS_EPS) * g


def _dot(a, b):
    return jnp.dot(a, b, preferred_element_type=F32)


def _dot_nt(a, b):
    return lax.dot_general(a, b, NT_DIMS, preferred_element_type=F32)


def _params(n_axes):
    return pltpu.CompilerParams(dimension_semantics=("arbitrary",) * n_axes, vmem_limit_bytes=VMEM_LIMIT)


def _front_kernel(*refs, d_conv, q_lora, kv_lora, seq_rows, prompt):
    if prompt:
        (x_ref, cs_ref, gmix_ref, win_ref, cw_ref, gq_ref, gkv_ref, wq_ref, gconv_ref,
         wk_ref, place_ref, wv_ref, vone_ref,
         u_ref, ycn_ref, q_ref, ckv_ref, kr_ref, k_ref, v_ref, carry_ref) = refs
    else:
        (x_ref, cs_ref, ov1_ref, ov2_ref, gmix_ref, win_ref, cw_ref, gq_ref, gkv_ref, wq_ref,
         gconv_ref, wukt_ref,
         u_ref, ycn_ref, q_ref, ckv_ref, kr_ref, qlat_ref) = refs
    tm = x_ref.shape[0]
    n = _rms(x_ref[...], gmix_ref[...]).astype(BF16)
    z = _dot(n, win_ref[...])
    c0 = 3 * d_conv
    b_g = z[:, 0:d_conv]
    u = z[:, d_conv:2 * d_conv] * z[:, 2 * d_conv:c0]
    z_q = z[:, c0:c0 + q_lora]
    z_kv = z[:, c0 + q_lora:c0 + q_lora + kv_lora]
    z_r = z[:, c0 + q_lora + kv_lora:]
    u_ref[...] = u

    row = lax.broadcasted_iota(jnp.int32, u.shape, 0)
    r1 = pltpu.roll(u, 1, 0)
    r2 = pltpu.roll(u, 2, 0)
    if prompt:
        @pl.when(pl.program_id(0) == 0)
        def _():
            carry_ref[...] = jnp.zeros_like(carry_ref)
        last1 = carry_ref[7:8, :]
        last2 = carry_ref[6:7, :]
        p1 = jnp.where(row == 0, last1, r1)
        p2 = jnp.where(row == 0, last2, jnp.where(row == 1, last1, r2))
        carry_ref[...] = u[tm - 8:tm, :]
    else:
        t = row % seq_rows
        p1 = jnp.where(t == 0, ov1_ref[...], r1)
        p2 = jnp.where(t < 2, ov2_ref[...], r2)
    cw = cw_ref[...]
    y_c = p2 * cw[0:1, :] + p1 * cw[1:2, :] + u * cw[2:3, :]
    ycn_ref[...] = _rms(b_g * y_c, gconv_ref[...]).astype(BF16)

    nq = _rms(z_q, gq_ref[...]).astype(BF16)
    q2 = _dot(nq, wq_ref[...])
    hw = N_HEADS * LANES
    cs = cs_ref[...]
    cos, sin = cs[:, :HALF_ROPE], cs[:, HALF_ROPE:]
    qs = SOFTMAX_SCALE * LOG2E
    pad = jnp.zeros((tm, LANES - QK_NOPE - QK_ROPE), F32)
    tqc = jnp.concatenate([jnp.full((tm, QK_NOPE), qs, F32), cos * qs, cos * qs, pad], axis=1)
    tqs = jnp.concatenate([jnp.zeros((tm, QK_NOPE), F32), sin * qs, sin * qs, pad], axis=1)
    q_heads = []
    for h in range(N_HEADS):
        qa = q2[:, h * LANES:(h + 1) * LANES]
        qb = q2[:, hw + h * LANES:hw + (h + 1) * LANES]
        q_heads.append((qa * tqc + qb * tqs).astype(BF16))
    q_ref[...] = jnp.concatenate(q_heads, axis=1)

    ckv = _rms(z_kv, gkv_ref[...])
    ckv_ref[...] = ckv
    tk = jnp.concatenate([cos, cos, sin, sin, jnp.zeros((tm, LANES - 2 * QK_ROPE), F32)], axis=1)
    zr = z_r * tk
    kr = zr[:, 0:QK_ROPE] + zr[:, QK_ROPE:2 * QK_ROPE]
    kr_ref[...] = kr

    if prompt:
        ckv_b = ckv.astype(BF16)
        k_ref[...] = (_dot(ckv_b, wk_ref[...]) + _dot(kr.astype(BF16), place_ref[...])).astype(BF16)
        v_ref[...] = (_dot(ckv_b, wv_ref[...]) + vone_ref[...]).astype(BF16)
    else:
        for h in range(N_HEADS):
            qlat_ref[:, h * kv_lora:(h + 1) * kv_lora] = _dot(q_heads[h], wukt_ref[h]).astype(BF16)


def _front(x, pos, w, *, prompt, ov=None, seq_rows=0):
    n, d = x.shape
    d_conv, q_lora, kv_lora = w["d_conv"], w["q_lora"], w["kv_lora"]
    tm = min(FRONT_ROWS, n)
    hw = N_HEADS * LANES
    inv_freq = ROPE_THETA ** (-jnp.arange(HALF_ROPE, dtype=F32) / HALF_ROPE)
    ang = pos.astype(F32)[:, None] * inv_freq[None, :]
    cs = jnp.concatenate([jnp.cos(ang), jnp.sin(ang)], axis=1)

    row = lambda c: pl.BlockSpec((tm, c), lambda i: (i, 0))
    full = lambda a: pl.BlockSpec(a.shape, lambda i: (0,) * a.ndim)
    ins = [x, cs]
    in_specs = [row(d), row(2 * HALF_ROPE)]
    if not prompt:
        ins += [ov[0], ov[1]]
        in_specs += [row(d_conv), row(d_conv)]
    shared = [w["g_mix"], w["w_in"], w["conv_w"], w["g_q"], w["g_kv"], w["wq2"], w["g_conv"]]
    shared += [w["wk_pad"], w["place"], w["wv_pad"], w["v_one"]] if prompt else [w["wuk_t"]]
    ins += shared
    in_specs += [full(a) for a in shared]

    outs = [jax.ShapeDtypeStruct((n, d_conv), F32), jax.ShapeDtypeStruct((n, d_conv), BF16),
            jax.ShapeDtypeStruct((n, hw), BF16), jax.ShapeDtypeStruct((n, kv_lora), F32),
            jax.ShapeDtypeStruct((n, QK_ROPE), F32)]
    out_specs = [row(d_conv), row(d_conv), row(hw), row(kv_lora), row(QK_ROPE)]
    scratch = []
    if prompt:
        outs += [jax.ShapeDtypeStruct((n, hw), BF16), jax.ShapeDtypeStruct((n, hw), BF16)]
        out_specs += [row(hw), row(hw)]
        scratch = [pltpu.VMEM((8, d_conv), F32)]
    else:
        outs += [jax.ShapeDtypeStruct((n, N_HEADS * kv_lora), BF16)]
        out_specs += [row(N_HEADS * kv_lora)]
    kern = functools.partial(_front_kernel, d_conv=d_conv, q_lora=q_lora, kv_lora=kv_lora, seq_rows=seq_rows,
                             prompt=prompt)
    return pl.pallas_call(
        kern, out_shape=outs, grid=(n // tm,), in_specs=in_specs, out_specs=out_specs, scratch_shapes=scratch,
        name="front_prompt" if prompt else "front_sample", compiler_params=_params(1),
    )(*ins)


def _flash_kernel(q_ref, k_ref, v_ref, o_ref, m_sc, acc_sc, *, tq):
    i = pl.program_id(1)
    m_sc[...] = jnp.full_like(m_sc, NEG)
    acc_sc[...] = jnp.zeros_like(acc_sc)
    half = tq // 2

    def block(row0, n_rows, key0, n_keys, diagonal):
        rows = slice(row0, row0 + n_rows)
        for h in range(2):
            sl = slice(h * LANES, (h + 1) * LANES)
            s = _dot_nt(q_ref[rows, sl], k_ref[pl.ds(key0, n_keys), sl])
            if diagonal:
                r = lax.broadcasted_iota(jnp.int32, s.shape, 0)
                c = lax.broadcasted_iota(jnp.int32, s.shape, 1)
                s = jnp.where(c <= r, s, NEG)
            m_prev = m_sc[h, rows]
            m_new = jnp.maximum(m_prev, jnp.max(s, axis=-1, keepdims=True))
            alpha = jnp.exp2(m_prev - m_new)
            p = jnp.exp2(s - jnp.concatenate([m_new] * (n_keys // LANES), axis=1))
            acc_sc[h, rows] = alpha * acc_sc[h, rows] + _dot(p.astype(BF16), v_ref[pl.ds(key0, n_keys), sl])
            m_sc[h, rows] = m_new

    def full_block(j):
        block(0, tq, pl.multiple_of(j * tq, tq), tq, False)

    per_trip = FLASH_TILES_PER_TRIP

    def body(jj, carry):
        for t in range(per_trip):
            full_block(per_trip * jj + t)
        return carry

    lax.fori_loop(0, i // per_trip, body, 0)
    done = (i // per_trip) * per_trip
    part = per_trip // 2
    while part >= 1:
        @pl.when((i - done) & part != 0)
        def _(done=done, part=part):
            for t in range(part):
                full_block(done + t)
        done = done + ((i - done) & part)
        part //= 2

    diag = pl.multiple_of(i * tq, tq)
    block(0, tq, diag, half, True)
    block(half, half, diag + half, half, True)
    outs = []
    for h in range(2):
        a = acc_sc[h]
        outs.append(a[:, :V_DIM] / a[:, V_DIM:V_DIM + 1])
    o_ref[...] = jnp.concatenate(outs, axis=1)


def _flash(q, k, v):
    s = q.shape[0]
    tq = min(FLASH_ROWS, s)
    pairs = N_HEADS // 2
    return pl.pallas_call(
        functools.partial(_flash_kernel, tq=tq),
        out_shape=jax.ShapeDtypeStruct((s, N_HEADS * V_DIM), F32),
        grid=(pairs, s // tq),
        in_specs=[pl.BlockSpec((tq, 2 * LANES), lambda p, i: (i, p)),
                  pl.BlockSpec((s, 2 * LANES), lambda p, i: (0, p)),
                  pl.BlockSpec((s, 2 * LANES), lambda p, i: (0, p))],
        out_specs=pl.BlockSpec((tq, 2 * V_DIM), lambda p, i: (i, p)),
        scratch_shapes=[pltpu.VMEM((2, tq, LANES), F32), pltpu.VMEM((2, tq, LANES), F32)],
        name="flash_prompt", compiler_params=_params(2),
    )(q, k, v)


def _paged_kernel(pt_ref, qlat_ref, qr_ref, cnew_ref, krnew_ref, cache_c, cache_krt, o_ref, cbuf, krbuf, cbf, sem,
                  *, nb, n_pages, page, ppc, t_new):
    b = pl.program_id(0)
    slot = b % 2
    n_chunks = n_pages // ppc
    ck = ppc * page

    def fetch(bb, sl):
        for pg_i in range(n_pages):
            pg = pt_ref[bb, pg_i]
            ci, p = divmod(pg_i, ppc)
            pltpu.make_async_copy(cache_c.at[pg], cbuf.at[sl, pg_i], sem.at[sl]).start()
            pltpu.make_async_copy(cache_krt.at[pg], krbuf.at[sl, ci, :, pl.ds(p * page, page)], sem.at[sl]).start()

    @pl.when(b == 0)
    def _():
        fetch(0, 0)

    if nb > 1:
        @pl.when(b + 1 < nb)
        def _():
            fetch(b + 1, 1 - slot)

    pltpu.make_async_copy(cache_c.at[pl.ds(0, n_pages)], cbuf.at[slot], sem.at[slot]).wait()
    pltpu.make_async_copy(krbuf.at[1 - slot], krbuf.at[slot], sem.at[slot]).wait()

    qlat = qlat_ref[...]
    qr = qr_ref[...]
    rows, kv_lora = qlat.shape

    def scores(ci):
        cb = cbuf[slot, pl.ds(ci * ppc, ppc)].reshape(ck, kv_lora).astype(BF16)
        cbf[ci % 2] = cb
        return _dot_nt(qlat, cb) + _dot(qr, krbuf[slot, ci].astype(BF16))

    def absorb(ci, s, m_prev, l_prev, acc):
        m_new = jnp.maximum(m_prev, jnp.max(s, axis=-1, keepdims=True))
        alpha = jnp.exp2(m_prev - m_new)
        p_ = jnp.exp2(s - m_new)
        l_new = alpha * l_prev + jnp.sum(p_, axis=-1, keepdims=True)
        return m_new, l_new, alpha * acc + _dot(p_.astype(BF16), cbf[ci % 2])

    def chunk(ci, carry):
        s, m_prev, l_prev, acc = carry
        s_next = scores(ci + 1)
        return (s_next,) + absorb(ci, s, m_prev, l_prev, acc)

    init = (scores(0), jnp.full((rows, 1), NEG, F32), jnp.zeros((rows, 1), F32), jnp.zeros((rows, kv_lora), F32))
    s_last, m_old, l_old, acc = lax.fori_loop(0, n_chunks - 1, chunk, init)
    m_old, l_old, acc = absorb(n_chunks - 1, s_last, m_old, l_old, acc)

    ql = qlat.astype(F32)
    qrf = qr.astype(F32)
    cn = cnew_ref[...]
    kn = krnew_ref[...]
    t_row = lax.broadcasted_iota(jnp.int32, (rows, 1), 0) // N_HEADS
    s_new = []
    for j in range(t_new):
        sj = (jnp.sum(ql * cn[j:j + 1, :], axis=-1, keepdims=True)
              + jnp.sum(qrf * kn[j:j + 1, :], axis=-1, keepdims=True))
        s_new.append(jnp.where(t_row >= j, sj, NEG))
    m_fin = m_old
    for sj in s_new:
        m_fin = jnp.maximum(m_fin, sj)
    a = jnp.exp2(m_old - m_fin)
    l_fin = a * l_old
    acc = a * acc
    for j, sj in enumerate(s_new):
        pj = jnp.exp2(sj - m_fin)
        l_fin = l_fin + pj
        acc = acc + pj * cn[j:j + 1, :]
    o_ref[...] = acc / l_fin


def _paged_attention(qlat, qr, c_new, kr_new, cache_c, cache_krt, page_table):
    nb, rows, kv_lora = qlat.shape
    n_pages = page_table.shape[1]
    page = cache_c.shape[1]
    t_new = c_new.shape[1]
    ppc = math.gcd(PAGES_PER_CHUNK, n_pages)
    per_b = lambda r, w: pl.BlockSpec((None, r, w), lambda b, pt: (b, 0, 0))
    any_spec = pl.BlockSpec(memory_space=pl.ANY)
    grid_spec = pltpu.PrefetchScalarGridSpec(
        num_scalar_prefetch=1, grid=(nb,),
        in_specs=[per_b(rows, kv_lora), per_b(rows, QK_ROPE), per_b(t_new, kv_lora), per_b(t_new, QK_ROPE),
                  any_spec, any_spec],
        out_specs=per_b(rows, kv_lora),
        scratch_shapes=[pltpu.VMEM((2, n_pages, page, kv_lora), F32),
                        pltpu.VMEM((2, n_pages // ppc, QK_ROPE, ppc * page), F32),
                        pltpu.VMEM((2, ppc * page, kv_lora), BF16),
                        pltpu.SemaphoreType.DMA((2,))])
    return pl.pallas_call(
        functools.partial(_paged_kernel, nb=nb, n_pages=n_pages, page=page, ppc=ppc, t_new=t_new),
        out_shape=jax.ShapeDtypeStruct((nb, rows, kv_lora), F32), grid_spec=grid_spec, name="paged_sample",
        compiler_params=_params(1),
    )(page_table, qlat, qr, c_new, kr_new, cache_c, cache_krt)


def _back_kernel(*refs, latent_values, sub):
    if latent_values:
        (h_ref, ycn_ref, ya_ref, wuv_ref, gattn_ref, woc_ref, woa_ref, gffn_ref, wrh_ref, wrl_ref, br_ref, tri_ref,
         hp_ref, xn_ref, route_ref, count_ref, base_sc) = refs
    else:
        (h_ref, ycn_ref, ya_ref, gattn_ref, woc_ref, woa_ref, gffn_ref, wrh_ref, wrl_ref, br_ref, tri_ref,
         hp_ref, xn_ref, route_ref, count_ref, base_sc) = refs

    @pl.when(pl.program_id(0) == 0)
    def _():
        base_sc[...] = jnp.zeros_like(base_sc)

    def sub_tile(rows, base):
        if latent_values:
            ya = _dot(ya_ref[rows, :].astype(BF16), wuv_ref[...])
        else:
            ya = ya_ref[rows, :]
        yan = _rms(ya, gattn_ref[...]).astype(BF16)
        hp = h_ref[rows, :] + _dot(ycn_ref[rows, :], woc_ref[...]) + _dot(yan, woa_ref[...])
        hp_ref[rows, :] = hp
        xn = _rms(hp, gffn_ref[...])
        xn_ref[rows, :] = xn
        x_hi = xn.astype(BF16)
        x_lo = (xn - x_hi.astype(F32)).astype(BF16)
        logits = _dot(x_hi, wrh_ref[...]) + _dot(x_hi, wrl_ref[...]) + _dot(x_lo, wrh_ref[...]) + br_ref[...]
        n_exp = logits.shape[1]
        lane = lax.broadcasted_iota(jnp.int32, logits.shape, 1).astype(F32)
        work = logits
        picks = []
        for _ in range(TOP_K):
            vmax = jnp.max(work, axis=-1, keepdims=True)
            idx = jnp.min(jnp.where(work == vmax, lane, float(n_exp)), axis=-1, keepdims=True)
            sel = lane == idx
            picks.append((vmax, idx, sel))
            work = jnp.where(sel, -jnp.inf, work)
        top = picks[0][0]
        exps = [jnp.exp(v - top) for v, _, _ in picks]
        denom = exps[0]
        for e in exps[1:]:
            denom = denom + e

        onehot = jnp.zeros_like(logits)
        for _, _, sel in picks:
            onehot = jnp.where(sel, 1.0, onehot)
        rank_all = _dot(tri_ref[...], onehot.astype(BF16)) + base

        lane_r = lax.broadcasted_iota(jnp.int32, (sub, LANES), 1)
        route = jnp.zeros((sub, LANES), F32)
        for k, (e, (_, idx, sel)) in enumerate(zip(exps, picks)):
            rank_k = jnp.sum(jnp.where(sel, rank_all, 0.0), axis=-1, keepdims=True)
            route = jnp.where(lane_r == ROUTE_EXPERT + k, idx, route)
            route = jnp.where(lane_r == ROUTE_RANK + k, rank_k, route)
            route = jnp.where(lane_r == ROUTE_GATE + k, e / denom, route)
        route_ref[rows, :] = route
        return base + jnp.sum(onehot, axis=0, keepdims=True)

    base = base_sc[...]
    for s in range(h_ref.shape[0] // sub):
        base = sub_tile(slice(s * sub, (s + 1) * sub), base)
    base_sc[...] = base
    count_ref[...] = base


def _back(h, ycn, ya, w, *, latent_values):
    n, d = h.shape
    tm = min(BACK_ROWS, n)
    sub = min(BACK_SUB_ROWS, tm)
    n_exp = w["wr_hi"].shape[1]
    row = lambda c: pl.BlockSpec((tm, c), lambda i: (i, 0))
    full = lambda a: pl.BlockSpec(a.shape, lambda i: (0,) * a.ndim)
    tri = jnp.tril(jnp.ones((sub, sub), F32), k=-1).astype(BF16)
    shared = [w["g_attn"], w["wo_conv"], w["wo_attn"], w["g_ffn"], w["wr_hi"], w["wr_lo"], w["b_router"], tri]
    if latent_values:
        shared = [w["w_uv_bd"]] + shared
    return pl.pallas_call(
        functools.partial(_back_kernel, latent_values=latent_values, sub=sub),
        out_shape=[jax.ShapeDtypeStruct((n, d), F32), jax.ShapeDtypeStruct((n, d), F32),
                   jax.ShapeDtypeStruct((n, LANES), F32), jax.ShapeDtypeStruct((1, n_exp), F32)],
        grid=(n // tm,),
        in_specs=[row(d), row(ycn.shape[1]), row(ya.shape[1])] + [full(a) for a in shared],
        out_specs=[row(d), row(d), row(LANES), pl.BlockSpec((1, n_exp), lambda i: (0, 0))],
        scratch_shapes=[pltpu.VMEM((1, n_exp), F32)],
        name="back_sample" if latent_values else "back_prompt", compiler_params=_params(1),
    )(h, ycn, ya, *shared)


def _dispatch_kernel(slot_ref, x_ref, xs_in_ref, xs_ref, sem):
    del xs_in_ref
    tm = x_ref.shape[0]

    def body(r, carry):
        for k in range(TOP_K):
            s = slot_ref[r * TOP_K + k]
            pltpu.make_async_copy(x_ref.at[pl.ds(r, 1)], xs_ref.at[pl.ds(s, 1)], sem).start(priority=k % 2)
        return carry

    lax.fori_loop(0, tm, body, 0, unroll=ROW_COPY_UNROLL)
    for k in range(TOP_K):
        pltpu.make_async_copy(x_ref, xs_ref.at[pl.ds(0, tm)], sem).wait()


def _dispatch(x, slots, xs):
    n, d = x.shape
    tm = min(DISPATCH_ROWS, n)
    return pl.pallas_call(
        _dispatch_kernel,
        out_shape=jax.ShapeDtypeStruct(xs.shape, F32),
        grid=(n // tm,),
        in_specs=[pl.BlockSpec((tm * TOP_K,), lambda i: (i,), memory_space=pltpu.SMEM),
                  pl.BlockSpec((tm, d), lambda i: (i, 0)),
                  pl.BlockSpec(memory_space=pl.ANY)],
        out_specs=pl.BlockSpec(memory_space=pl.ANY),
        scratch_shapes=[pltpu.SemaphoreType.DMA(())],
        input_output_aliases={2: 0},
        name="moe_dispatch", compiler_params=_params(1),
    )(slots, x, xs)


def _expert_kernel(te_ref, nx_ref, vr_ref, nu_ref, xs_ref, bgu_ref, bd_ref, wgu_hbm, wd_hbm, ys_ref,
                   wgu_st, wd_st, wgu_bf, wd_bf, sem):
    ti = pl.program_id(0)
    used = ti < nu_ref[0]
    new_expert = (ti == 0) | (te_ref[ti] != te_ref[jnp.maximum(ti - 1, 0)])
    d_ff = wd_bf.shape[0]

    def weight_copies(e):
        return (pltpu.make_async_copy(wgu_hbm.at[e], wgu_st, sem.at[0]),
                pltpu.make_async_copy(wd_hbm.at[e], wd_st, sem.at[1]))

    @pl.when(ti == 0)
    def _():
        for cp in weight_copies(te_ref[0]):
            cp.start()

    @pl.when(used & new_expert)
    def _():
        for cp in weight_copies(te_ref[ti]):
            cp.wait()
        wgu_bf[...] = wgu_st[...].astype(BF16)
        wd_bf[...] = wd_st[...].astype(BF16)

        @pl.when(nx_ref[ti] >= 0)
        def _():
            for cp in weight_copies(nx_ref[ti]):
                cp.start()

    def mlp(x):
        x = x.astype(BF16)
        bgu = bgu_ref[...]
        gate = jnp.minimum(_dot(x, wgu_bf[:, :d_ff]) + bgu[:, :d_ff], SWIGLU_LIMIT)
        up = jnp.clip(_dot(x, wgu_bf[:, d_ff:]) + bgu[:, d_ff:], -SWIGLU_LIMIT, SWIGLU_LIMIT)
        hid = (up + 1.0) * gate * jax.nn.sigmoid(SWIGLU_ALPHA * gate)
        return _dot(hid.astype(BF16), wd_bf[...]) + bd_ref[...]

    half = xs_ref.shape[0] // 2
    valid = vr_ref[ti]

    @pl.when(used & (valid > half))
    def _():
        ys_ref[...] = mlp(xs_ref[...])

    @pl.when(used & (valid <= half))
    def _():
        ys_ref[:half, :] = mlp(xs_ref[:half, :])
        ys_ref[half:, :] = jnp.zeros((half, ys_ref.shape[1]), F32)

    @pl.when(jnp.logical_not(used))
    def _():
        ys_ref[...] = jnp.zeros_like(ys_ref)


def _experts(xs, tile_expert, next_expert, valid_rows, n_used, w_gu, b_gu, w_d, b_d):
    n_slots, d = xs.shape
    n_exp, _, d_ff2 = w_gu.shape
    d_ff = d_ff2 // 2
    tmm = MOE_TILE
    rows = lambda t, te, nx, vr, nu: (jnp.minimum(t, nu[0] - 1), 0)
    any_spec = pl.BlockSpec(memory_space=pl.ANY)
    grid_spec = pltpu.PrefetchScalarGridSpec(
        num_scalar_prefetch=4, grid=(n_slots // tmm,),
        in_specs=[pl.BlockSpec((tmm, d), rows),
                  pl.BlockSpec((None, 1, d_ff2), lambda t, te, nx, vr, nu: (te[t], 0, 0)),
                  pl.BlockSpec((None, 1, d), lambda t, te, nx, vr, nu: (te[t], 0, 0)),
                  any_spec, any_spec],
        out_specs=pl.BlockSpec((tmm, d), lambda t, te, nx, vr, nu: (t, 0)),
        scratch_shapes=[pltpu.VMEM((d, d_ff2), F32), pltpu.VMEM((d_ff, d), F32),
                        pltpu.VMEM((d, d_ff2), BF16), pltpu.VMEM((d_ff, d), BF16),
                        pltpu.SemaphoreType.DMA((2,))])
    return pl.pallas_call(
        _expert_kernel, out_shape=jax.ShapeDtypeStruct((n_slots, d), F32), grid_spec=grid_spec,
        name="moe_experts", compiler_params=_params(1),
    )(tile_expert, next_expert, valid_rows, n_used, xs, b_gu.reshape(n_exp, 1, d_ff2), b_d.reshape(n_exp, 1, d),
      w_gu, w_d)


def _combine_kernel(slot_ref, next_slot_ref, route_ref, hp_ref, ys_ref, gfin_ref, o_ref, buf, sem, *, steps):
    i = pl.program_id(0)
    tm = hp_ref.shape[0]

    def gather(slots, b):
        def body(r, carry):
            for k in range(TOP_K):
                s = slots[r * TOP_K + k]
                pltpu.make_async_copy(ys_ref.at[pl.ds(s, 1)], buf.at[b, k, pl.ds(r, 1)],
                                      sem.at[b]).start(priority=k % 2)
            return carry

        lax.fori_loop(0, tm, body, 0, unroll=ROW_COPY_UNROLL)

    @pl.when(i == 0)
    def _():
        gather(slot_ref, 0)

    if steps > 1:
        @pl.when(i + 1 < steps)
        def _():
            gather(next_slot_ref, (i + 1) % 2)

    b = i % 2
    for k in range(TOP_K):
        pltpu.make_async_copy(ys_ref.at[pl.ds(0, tm)], buf.at[b, k], sem.at[b]).wait()
    route = route_ref[...]
    acc = hp_ref[...]
    for k in range(TOP_K):
        acc = acc + route[:, ROUTE_GATE + k:ROUTE_GATE + k + 1] * buf[b, k]
    o_ref[...] = _rms(acc, gfin_ref[...])


def _combine(slots, route, hp, ys, g_fin):
    n, d = hp.shape
    tm = min(COMBINE_ROWS, n)
    steps = n // tm
    return pl.pallas_call(
        functools.partial(_combine_kernel, steps=steps),
        out_shape=jax.ShapeDtypeStruct((n, d), F32),
        grid=(steps,),
        in_specs=[pl.BlockSpec((tm * TOP_K,), lambda i: (i,), memory_space=pltpu.SMEM),
                  pl.BlockSpec((tm * TOP_K,), lambda i: (jnp.minimum(i + 1, steps - 1),), memory_space=pltpu.SMEM),
                  pl.BlockSpec((tm, LANES), lambda i: (i, 0)),
                  pl.BlockSpec((tm, d), lambda i: (i, 0)),
                  pl.BlockSpec(memory_space=pl.ANY),
                  pl.BlockSpec((1, d), lambda i: (0, 0))],
        out_specs=pl.BlockSpec((tm, d), lambda i: (i, 0)),
        scratch_shapes=[pltpu.VMEM((2, TOP_K, tm, d), F32), pltpu.SemaphoreType.DMA((2,))],
        name="moe_combine", compiler_params=_params(1),
    )(slots, slots, route, hp, ys, g_fin)


def _slot_tables(routes, counts, n_exp):
    tmm = MOE_TILE
    counts = [c.reshape(n_exp).astype(jnp.int32) for c in counts]
    total = sum(counts)
    group = ((total + tmm - 1) // tmm) * tmm
    ends = jnp.cumsum(group)
    start = ends - group
    slots = []
    before = jnp.zeros_like(total)
    experts = jnp.arange(n_exp, dtype=jnp.int32)
    for route, cnt in zip(routes, counts):
        eid = route[:, ROUTE_EXPERT:ROUTE_EXPERT + TOP_K].astype(jnp.int32)
        rank = route[:, ROUTE_RANK:ROUTE_RANK + TOP_K].astype(jnp.int32)
        first = jnp.sum(jnp.where(eid[:, :, None] == experts, start + before, 0), axis=-1)
        slots.append((first + rank).reshape(-1))
        before = before + cnt
    n_tokens = sum(r.shape[0] for r in routes)
    max_tiles = (n_tokens * TOP_K) // tmm + n_exp
    n_used = ends[-1] // tmm
    tile_start = jnp.minimum(jnp.arange(max_tiles, dtype=jnp.int32) * tmm, ends[-1] - 1)
    tile_expert = jnp.sum((ends[None, :] <= tile_start[:, None]).astype(jnp.int32), axis=1)
    group_end = jnp.sum(jnp.where(tile_expert[:, None] == experts, ends, 0), axis=1)
    following = jnp.sum((ends[None, :] <= group_end[:, None]).astype(jnp.int32), axis=1)
    next_expert = jnp.where(group_end < ends[-1], following, -1).astype(jnp.int32)
    filled_end = jnp.sum(jnp.where(tile_expert[:, None] == experts, start + total, 0), axis=1)
    valid_rows = jnp.clip(filled_end - jnp.arange(max_tiles, dtype=jnp.int32) * tmm, 0, tmm).astype(jnp.int32)
    return (slots, tile_expert, next_expert, valid_rows, n_used.reshape(1).astype(jnp.int32), max_tiles * tmm)


def _prep_layer(norm_mix_g, w_in, conv_w, q_norm_g, kv_norm_g, w_uq, w_ukv, out_norm_conv_g, out_norm_attn_g,
                w_out, norm_ffn_g, w_router, b_router):
    d = w_in.shape[0]
    d_conv = conv_w.shape[1]
    q_lora = q_norm_g.shape[0]
    kv_lora = kv_norm_g.shape[0]
    c_kr = 3 * d_conv + q_lora + kv_lora
    w_kr = w_in[:, c_kr:c_kr + QK_ROPE]
    w_kr_rot = jnp.concatenate([-w_kr[:, HALF_ROPE:], w_kr[:, :HALF_ROPE]], axis=1)
    w_in_pad = jnp.concatenate([w_in[:, :c_kr], w_kr, w_kr_rot, jnp.zeros((d, LANES - 2 * QK_ROPE), F32)], axis=1)

    qk = QK_NOPE + QK_ROPE
    wq = w_uq.reshape(q_lora, N_HEADS, qk)
    pad = jnp.zeros((q_lora, N_HEADS, LANES - qk), F32)
    wq_pad = jnp.concatenate([wq, pad], axis=2)
    r1 = wq[:, :, QK_NOPE:QK_NOPE + HALF_ROPE]
    r2 = wq[:, :, QK_NOPE + HALF_ROPE:]
    wq_rot = jnp.concatenate([jnp.zeros((q_lora, N_HEADS, QK_NOPE), F32), -r2, r1, pad], axis=2)
    wq2 = jnp.concatenate([wq_pad.reshape(q_lora, -1), wq_rot.reshape(q_lora, -1)], axis=1)

    wkv = w_ukv.reshape(kv_lora, N_HEADS, QK_NOPE + V_DIM)
    w_uk = wkv[:, :, :QK_NOPE]
    w_uv = wkv[:, :, QK_NOPE:]
    wk_pad = jnp.concatenate([w_uk, jnp.zeros((kv_lora, N_HEADS, LANES - QK_NOPE), F32)], axis=2)
    wv_pad = jnp.concatenate([w_uv, jnp.zeros((kv_lora, N_HEADS, LANES - V_DIM), F32)], axis=2)
    v_one = jnp.zeros((1, N_HEADS, LANES), F32).at[:, :, V_DIM].set(1.0)
    place = jnp.zeros((QK_ROPE, N_HEADS, LANES), F32)
    place = place.at[jnp.arange(QK_ROPE), :, QK_NOPE + jnp.arange(QK_ROPE)].set(1.0)
    wuk_t = jnp.concatenate([jnp.transpose(w_uk, (1, 2, 0)),
                             jnp.zeros((N_HEADS, LANES - QK_NOPE, kv_lora), F32)], axis=1)
    eye = jnp.eye(N_HEADS, dtype=F32)
    w_uv_bd = jnp.einsum("chv,hg->hcgv", w_uv, eye).reshape(N_HEADS * kv_lora, N_HEADS * V_DIM)
    wr_hi = w_router.astype(BF16)
    wr_lo = (w_router - wr_hi.astype(F32)).astype(BF16)
    return {
        "d_conv": d_conv, "q_lora": q_lora, "kv_lora": kv_lora,
        "g_mix": norm_mix_g.reshape(1, -1), "w_in": w_in_pad.astype(BF16), "conv_w": conv_w,
        "g_q": q_norm_g.reshape(1, -1), "g_kv": kv_norm_g.reshape(1, -1), "wq2": wq2.astype(BF16),
        "g_conv": out_norm_conv_g.reshape(1, -1), "wk_pad": wk_pad.reshape(kv_lora, -1).astype(BF16),
        "place": place.reshape(QK_ROPE, -1).astype(BF16), "wv_pad": wv_pad.reshape(kv_lora, -1).astype(BF16),
        "v_one": v_one.reshape(1, -1), "wuk_t": wuk_t.astype(BF16),
        "g_attn": out_norm_attn_g.reshape(1, -1), "wo_conv": w_out[:d_conv, :].astype(BF16),
        "wo_attn": w_out[d_conv:, :].astype(BF16), "w_uv_bd": w_uv_bd.astype(BF16),
        "g_ffn": norm_ffn_g.reshape(1, -1), "wr_hi": wr_hi, "wr_lo": wr_lo, "b_router": b_router.reshape(1, -1),
    }


def kernel(x_prompt, x_sample, cache_kv_latent, cache_k_rope, state_conv, page_table, norm_mix_g, w_in, conv_w,
           q_norm_g, kv_norm_g, w_uq, w_ukv, out_norm_conv_g, out_norm_attn_g, w_out, norm_ffn_g, w_router,
           b_router, w_gate_up, b_gate_up, w_down, b_down, final_norm_g):
    depth = w_in.shape[0]
    batch, seq, d = x_prompt.shape
    dec_b, dec_t, _ = x_sample.shape
    n_exp = w_router.shape[-1]
    past_len = page_table.shape[1] * cache_kv_latent.shape[2]
    assert batch == 1, "the prompt path lays one sequence out along rows"
    assert depth == 1, "deeper stacks need the un-normalised residual between layers"
    pos_p = jnp.arange(seq, dtype=jnp.int32)
    pos_s = jnp.tile(past_len + jnp.arange(dec_t, dtype=jnp.int32), dec_b)
    hp = x_prompt.reshape(seq, d)
    hs = x_sample.reshape(dec_b * dec_t, d)
    g_fin = final_norm_g.reshape(1, -1)
    outs = [[] for _ in range(6)]
    for l in range(depth):
        w = _prep_layer(norm_mix_g[l], w_in[l], conv_w[l], q_norm_g[l], kv_norm_g[l], w_uq[l], w_ukv[l],
                        out_norm_conv_g[l], out_norm_attn_g[l], w_out[l], norm_ffn_g[l], w_router[l], b_router[l])
        d_conv, kv_lora = w["d_conv"], w["kv_lora"]

        u_p, ycn_p, q_p, ckv_p, kr_p, k_p, v_p = _front(hp, pos_p, w, prompt=True)
        ya_p = _flash(q_p, k_p, v_p)
        hp_mid, xn_p, route_p, count_p = _back(hp, ycn_p, ya_p, w, latent_values=False)

        st = state_conv[l]
        t_idx = jnp.arange(dec_t)
        ov1 = jnp.where((t_idx == 0)[None, :, None], st[:, 1:2, :], 0.0).reshape(dec_b * dec_t, d_conv)
        ov2 = jnp.where((t_idx == 0)[None, :, None], st[:, 0:1, :],
                        jnp.where((t_idx == 1)[None, :, None], st[:, 1:2, :], 0.0)).reshape(dec_b * dec_t, d_conv)
        u_s, ycn_s, q_s, ckv_s, kr_s, qlat_s = _front(hs, pos_s, w, prompt=False, ov=(ov1, ov2), seq_rows=dec_t)
        rows = dec_t * N_HEADS
        qr_s = q_s.reshape(dec_b, rows, LANES)[:, :, QK_NOPE:QK_NOPE + QK_ROPE]
        cache_krt = jnp.swapaxes(cache_k_rope[l], 1, 2)
        o_lat = _paged_attention(qlat_s.reshape(dec_b, rows, kv_lora), qr_s,
                                 ckv_s.reshape(dec_b, dec_t, kv_lora), kr_s.reshape(dec_b, dec_t, QK_ROPE),
                                 cache_kv_latent[l], cache_krt, page_table)
        ya_s = o_lat.reshape(dec_b * dec_t, N_HEADS * kv_lora)
        hs_mid, xn_s, route_s, count_s = _back(hs, ycn_s, ya_s, w, latent_values=True)

        (slot_p, slot_s), tile_expert, next_expert, valid_rows, n_used, n_slots = _slot_tables(
            (route_p, route_s), (count_p, count_s), n_exp)
        xs = jnp.zeros((n_slots, d), F32)
        xs = _dispatch(xn_p, slot_p, xs)
        xs = _dispatch(xn_s, slot_s, xs)
        ys = _experts(xs, tile_expert, next_expert, valid_rows, n_used,
                      w_gate_up[l], b_gate_up[l], w_down[l], b_down[l])
        hp = _combine(slot_p, route_p, hp_mid, ys, g_fin)
        hs = _combine(slot_s, route_s, hs_mid, ys, g_fin)

        outs[0].append(ckv_p.reshape(batch, seq, kv_lora))
        outs[1].append(kr_p.reshape(batch, seq, QK_ROPE))
        outs[2].append(u_p[seq - 2:, :].reshape(batch, 2, d_conv))
        outs[3].append(ckv_s.reshape(dec_b, dec_t, kv_lora))
        outs[4].append(kr_s.reshape(dec_b, dec_t, QK_ROPE))
        outs[5].append(u_s.reshape(dec_b, dec_t, d_conv)[:, dec_t - 2:, :])
    return (hp.reshape(batch, seq, d), hs.reshape(dec_b, dec_t, d)) + tuple(jnp.stack(o) for o in outs)
```
